```python
import math
import jax
import jax.numpy as jnp
from jax import lax
import numpy as np

D_MODEL = 1024
BATCH = 8
SEQ = 2048
DEPTH = 1

N_ATTN_HEADS = 8
HEAD_DIM = 64
D_ATTN = N_ATTN_HEADS * HEAD_DIM
MOBA_BLOCK = 256
MOBA_TOPK = 3
MOBA_QUERY_CHUNK = 32
SSM_GROUP = 16
D_SSM = D_MODEL - D_ATTN
N_SSM_GROUPS = D_SSM // SSM_GROUP
SSM_STATE = 64
DT_MIN = 1e-3
DT_MAX = 1e-1
D_MIX = D_ATTN + D_SSM
D_IN = 3 * D_ATTN + D_SSM
D_FF = 256 * ((8 * D_MODEL // 3 + 255) // 256)
RMS_EPS = 1e-6
NEG_INF = -1e30

kernel_name = 'hymba_moba_s5_macaron_layer'


def rmsnorm(x, g):
    xf = x.astype(jnp.float32)
    y = xf * lax.rsqrt(jnp.mean(xf * xf, axis=-1, keepdims=True) + RMS_EPS)
    return (y * g.astype(jnp.float32)).astype(x.dtype)


def swiglu(x, w_gate, w_up, w_down):
    return (jax.nn.silu(x @ w_gate) * (x @ w_up)) @ w_down


def alibi_slopes(n_heads):
    return jnp.asarray(2.0 ** (-8.0 * np.arange(1, n_heads + 1) / n_heads), dtype=jnp.float32)


def moba_attention(q, k, v):
    bsz, s, nh, dh = q.shape
    L = MOBA_BLOCK
    C = MOBA_QUERY_CHUNK
    nb = -(-s // L)
    sp = nb * L
    nc = sp // C
    topk = min(MOBA_TOPK, nb)
    scale = 1.0 / math.sqrt(dh)
    slopes = alibi_slopes(nh)
    padw = ((0, 0), (0, sp - s), (0, 0), (0, 0))
    q, k, v = (jnp.pad(t, padw).transpose(0, 2, 1, 3) for t in (q, k, v))
    kb = k.reshape(bsz, nh, nb, L, dh)
    vb = v.reshape(bsz, nh, nb, L, dh)

    qblk = jnp.arange(sp) // L
    kmean = jnp.mean(kb.astype(jnp.float32), axis=3)
    gate = jnp.einsum('bhtd,bhnd->bhtn', q.astype(jnp.float32), kmean)
    past = jnp.arange(nb)[None, :] < qblk[:, None]
    gate = jnp.where(past, gate, NEG_INF)
    _, sel = lax.top_k(gate, topk)

    q_c = q.reshape(bsz, nh, nc, C, dh).transpose(2, 0, 1, 3, 4)
    sel_c = sel.reshape(bsz, nh, nc, C, topk).transpose(2, 0, 1, 3, 4)
    b_ix = jnp.arange(bsz)[:, None, None, None]
    h_ix = jnp.arange(nh)[None, :, None, None]
    key_off = jnp.arange(L)

    def chunk(args):
        qc, selc, ci = args
        t = ci * C + jnp.arange(C)
        own = (ci * C) // L
        k_sel = kb[b_ix, h_ix, selc]
        v_sel = vb[b_ix, h_ix, selc]
        k_own = lax.dynamic_index_in_dim(kb, own, axis=2, keepdims=False)
        v_own = lax.dynamic_index_in_dim(vb, own, axis=2, keepdims=False)
        s_sel = jnp.einsum('bhcd,bhckld->bhckl', qc, k_sel).astype(jnp.float32) * scale
        dist_sel = (t[None, None, :, None, None] - (selc[..., None] * L + key_off)).astype(jnp.float32)
        s_sel = s_sel - slopes[None, :, None, None, None] * dist_sel
        slot_ok = jnp.arange(topk)[None, :] < (t // L)[:, None]
        s_sel = jnp.where(slot_ok[None, None, :, :, None], s_sel, NEG_INF)
        s_own = jnp.einsum('bhcd,bhld->bhcl', qc, k_own).astype(jnp.float32) * scale
        dist_own = t[:, None] - (own * L + key_off)[None, :]
        s_own = jnp.where(dist_own >= 0,
                          s_own - slopes[None, :, None, None] * dist_own.astype(jnp.float32),
                          NEG_INF)
        scores = jnp.concatenate([s_sel.reshape(bsz, nh, C, topk * L), s_own], axis=-1)
        p = jax.nn.softmax(scores, axis=-1).astype(v_sel.dtype)
        p_sel = p[..., :topk * L].reshape(bsz, nh, C, topk, L)
        p_own = p[..., topk * L:]
        return (jnp.einsum('bhckl,bhckld->bhcd', p_sel, v_sel)
                + jnp.einsum('bhcl,bhld->bhcd', p_own, v_own))

    out = lax.map(chunk, (q_c, sel_c, jnp.arange(nc)))
    out = out.transpose(1, 2, 0, 3, 4).reshape(bsz, nh, sp, dh)[:, :, :s]
    return out.transpose(0, 2, 1, 3)


def s5_ssm(u, lam_re, lam_im, log_dt, b_re, b_im, c_re, c_im, d_skip, w_glu, b_glu):
    bsz, s, _ = u.shape
    f32 = jnp.float32
    uf = u.astype(f32).reshape(bsz, s, N_SSM_GROUPS, SSM_GROUP)
    lr = lam_re.astype(f32)
    li = lam_im.astype(f32)
    dt = jnp.exp(log_dt.astype(f32))[:, None]
    mag = jnp.exp(lr * dt)
    ang = li * dt
    a_re = mag * jnp.cos(ang)
    a_im = mag * jnp.sin(ang)
    den = lr * lr + li * li
    f_re = ((a_re - 1.0) * lr + a_im * li) / den
    f_im = (a_im * lr - (a_re - 1.0) * li) / den
    br = b_re.astype(f32)
    bi = b_im.astype(f32)
    bb_re = f_re[..., None] * br - f_im[..., None] * bi
    bb_im = f_re[..., None] * bi + f_im[..., None] * br
    x_re = jnp.einsum('bsgh,gph->bsgp', uf, bb_re)
    x_im = jnp.einsum('bsgh,gph->bsgp', uf, bb_im)
    a_re_s = jnp.broadcast_to(a_re, (1, s) + a_re.shape)
    a_im_s = jnp.broadcast_to(a_im, (1, s) + a_im.shape)

    def combine(e1, e2):
        a1r, a1i, x1r, x1i = e1
        a2r, a2i, x2r, x2i = e2
        return (a1r * a2r - a1i * a2i,
                a1r * a2i + a1i * a2r,
                a2r * x1r - a2i * x1i + x2r,
                a2r * x1i + a2i * x1r + x2i)

    _, _, h_re, h_im = lax.associative_scan(combine, (a_re_s, a_im_s, x_re, x_im), axis=1)
    y = (jnp.einsum('bsgp,ghp->bsgh', h_re, c_re.astype(f32))
         - jnp.einsum('bsgp,ghp->bsgh', h_im, c_im.astype(f32))
         + d_skip.astype(f32) * uf)
    y = jax.nn.gelu(y.reshape(bsz, s, D_SSM))
    y = y * jax.nn.sigmoid(y @ w_glu.astype(f32) + b_glu.astype(f32))
    return y.astype(u.dtype)


def setup_inputs(seed: int = 0) -> dict:
    key = jax.random.key(seed)
    ks = jax.random.split(key, 32)
    f32 = jnp.float32
    G, P, Hg = N_SSM_GROUPS, SSM_STATE, SSM_GROUP

    def nrm(k, shape, scale):
        return jax.random.normal(k, shape, f32) * scale

    def gain(k, n):
        return 1.0 + 0.02 * jax.random.normal(k, (DEPTH, n), f32)

    lam_re = -0.5 + 0.01 * jax.random.normal(ks[8], (DEPTH, G, P), f32)
    lam_im = math.pi * jnp.arange(P, dtype=f32)[None, None, :] + 0.01 * jax.random.normal(ks[9], (DEPTH, G, P), f32)
    log_dt = jax.random.uniform(ks[10], (DEPTH, G), f32, math.log(DT_MIN), math.log(DT_MAX))
    return {
        'x': jax.random.normal(ks[0], (BATCH, SEQ, D_MODEL), f32),
        'ffn1_pre_g': gain(ks[1], D_MODEL),
        'ffn1_w_gate': nrm(ks[2], (DEPTH, D_MODEL, D_FF), D_MODEL ** -0.5),
        'ffn1_w_up': nrm(ks[3], (DEPTH, D_MODEL, D_FF), D_MODEL ** -0.5),
        'ffn1_w_down': nrm(ks[4], (DEPTH, D_FF, D_MODEL), D_FF ** -0.5),
        'ffn1_post_g': gain(ks[5], D_MODEL),
        'mix_pre_g': gain(ks[6], D_MODEL),
        'w_in': nrm(ks[7], (DEPTH, D_MODEL, D_IN), D_MODEL ** -0.5),
        'lam_re': lam_re,
        'lam_im': lam_im,
        'log_dt': log_dt,
        'b_re': nrm(ks[11], (DEPTH, G, P, Hg), (2.0 * Hg) ** -0.5),
        'b_im': nrm(ks[12], (DEPTH, G, P, Hg), (2.0 * Hg) ** -0.5),
        'c_re': nrm(ks[13], (DEPTH, G, Hg, P), (2.0 * P) ** -0.5),
        'c_im': nrm(ks[14], (DEPTH, G, Hg, P), (2.0 * P) ** -0.5),
        'd_skip': nrm(ks[15], (DEPTH, G, Hg), 1.0),
        'w_glu': nrm(ks[16], (DEPTH, D_SSM, D_SSM), D_SSM ** -0.5),
        'b_glu': nrm(ks[17], (DEPTH, D_SSM), 0.01),
        'attn_out_g': gain(ks[18], D_ATTN),
        'ssm_out_g': gain(ks[19], D_SSM),
        'w_out': nrm(ks[20], (DEPTH, D_MIX, D_MODEL), D_MIX ** -0.5),
        'mix_post_g': gain(ks[21], D_MODEL),
        'ffn2_pre_g': gain(ks[22], D_MODEL),
        'ffn2_w_gate': nrm(ks[23], (DEPTH, D_MODEL, D_FF), D_MODEL ** -0.5),
        'ffn2_w_up': nrm(ks[24], (DEPTH, D_MODEL, D_FF), D_MODEL ** -0.5),
        'ffn2_w_down': nrm(ks[25], (DEPTH, D_FF, D_MODEL), D_FF ** -0.5),
        'ffn2_post_g': gain(ks[26], D_MODEL),
    }


def reference(x, ffn1_pre_g, ffn1_w_gate, ffn1_w_up, ffn1_w_down, ffn1_post_g,
              mix_pre_g, w_in, lam_re, lam_im, log_dt, b_re, b_im, c_re, c_im, d_skip,
              w_glu, b_glu, attn_out_g, ssm_out_g, w_out, mix_post_g,
              ffn2_pre_g, ffn2_w_gate, ffn2_w_up, ffn2_w_down, ffn2_post_g):
    bsz, s, _ = x.shape
    h = x
    for l in range(DEPTH):
        f = swiglu(rmsnorm(h, ffn1_pre_g[l]), ffn1_w_gate[l], ffn1_w_up[l], ffn1_w_down[l])
        h = h + 0.5 * rmsnorm(f, ffn1_post_g[l])
        u = rmsnorm(h, mix_pre_g[l])
        proj = u @ w_in[l]
        q = proj[..., :D_ATTN].reshape(bsz, s, N_ATTN_HEADS, HEAD_DIM)
        k = proj[..., D_ATTN:2 * D_ATTN].reshape(bsz, s, N_ATTN_HEADS, HEAD_DIM)
        v = proj[..., 2 * D_ATTN:3 * D_ATTN].reshape(bsz, s, N_ATTN_HEADS, HEAD_DIM)
        us = proj[..., 3 * D_ATTN:]
        attn = moba_attention(q, k, v).reshape(bsz, s, D_ATTN)
        ssm = s5_ssm(us, lam_re[l], lam_im[l], log_dt[l], b_re[l], b_im[l], c_re[l], c_im[l],
                     d_skip[l], w_glu[l], b_glu[l])
        merged = jnp.concatenate([rmsnorm(attn, attn_out_g[l]), rmsnorm(ssm, ssm_out_g[l])], axis=-1)
        h = h + rmsnorm(merged @ w_out[l], mix_post_g[l])
        f = swiglu(rmsnorm(h, ffn2_pre_g[l]), ffn2_w_gate[l], ffn2_w_up[l], ffn2_w_down[l])
        h = h + 0.5 * rmsnorm(f, ffn2_post_g[l])
    return h
```

```python
import functools
import math

import numpy as np
import jax
import jax.numpy as jnp
from jax import lax
from jax.experimental import pallas as pl
from jax.experimental.pallas import tpu as pltpu

F32 = jnp.float32
BF16 = jnp.bfloat16

N_HEADS = 8
HEAD_DIM = 64
D_ATTN = N_HEADS * HEAD_DIM
MOBA_BLOCK = 256
MOBA_TOPK = 3
SSM_GROUP = 16
SSM_STATE = 64
RMS_EPS = 1e-6
NEG_INF = -1e30

LANES = 128
SUBLANES = 8
HEADS_PER_LANE_TILE = LANES // HEAD_DIM
GROUPS_PER_LANE_TILE = LANES // SSM_GROUP
PAIRS_PER_LANE_TILE = GROUPS_PER_LANE_TILE // 2
STATE_COLS_PER_LANE_TILE = GROUPS_PER_LANE_TILE * 2 * SSM_STATE

ROW_TILE = 512
SCAN_STEPS = 64
SCAN_PAIRS_PER_LOOP = 4
VMEM_LIMIT_BYTES = 56 * 1024 * 1024


def _rms(x, g):
    return x * lax.rsqrt(jnp.mean(x * x, axis=-1, keepdims=True) + RMS_EPS) * g


def _ff_chunks(d_ff, width=768):
    return [(s, min(width, d_ff - s)) for s in range(0, d_ff, width)]


def _swiglu(xn, wg_ref, wu_ref, wd_ref, act_ref):
    for s, w in _ff_chunks(wg_ref.shape[1]):
        g = jnp.dot(xn, wg_ref[:, s:s + w], preferred_element_type=F32)
        u = jnp.dot(xn, wu_ref[:, s:s + w], preferred_element_type=F32)
        act_ref[:, s:s + w] = (g * jax.nn.sigmoid(g) * u).astype(BF16)
    return jnp.dot(act_ref[...], wd_ref[...], preferred_element_type=F32)


def _ffn1_proj_kernel(x_ref, pre_g, wg, wu, wd, post_g, mix_g, win,
                      h_ref, qt_ref, k_ref, vt_ref, us_ref, kmean_ref, act_ref):
    x = x_ref[...]
    f = _swiglu(_rms(x, pre_g[...]).astype(BF16), wg, wu, wd, act_ref)
    h = x + 0.5 * _rms(f, post_g[...])
    h_ref[...] = h
    u = _rms(h, mix_g[...]).astype(BF16)
    q = jnp.dot(u, win[:, 0:D_ATTN], preferred_element_type=F32) * (1.0 / math.sqrt(HEAD_DIM))
    k = jnp.dot(u, win[:, D_ATTN:2 * D_ATTN], preferred_element_type=F32)
    v = jnp.dot(u, win[:, 2 * D_ATTN:3 * D_ATTN], preferred_element_type=F32)
    us_ref[...] = jnp.dot(u, win[:, 3 * D_ATTN:], preferred_element_type=F32)
    k_ref[...] = k.astype(BF16)
    for blk in range(x.shape[0] // MOBA_BLOCK):
        rows = slice(blk * MOBA_BLOCK, (blk + 1) * MOBA_BLOCK)
        qt_ref[0, blk] = q[rows].T.astype(BF16)
        vt_ref[0, blk] = v[rows].T.astype(BF16)
        kmean_ref[0, blk:blk + 1, :] = jnp.mean(k[rows], axis=0, keepdims=True)


def _ffn1_proj(x2, pre_g, wg, wu, wd, post_g, mix_g, win, bsz, seq):
    n, d = x2.shape
    d_ff = wg.shape[1]
    d_in = win.shape[1]
    d_ssm = d_in - 3 * D_ATTN
    tm = ROW_TILE
    tiles_per_seq = seq // tm
    blocks_per_tile = tm // MOBA_BLOCK
    nb = seq // MOBA_BLOCK
    const = lambda shape: pl.BlockSpec(shape, lambda i: (0,) * len(shape), pipeline_mode=pl.Buffered(1))
    return pl.pallas_call(
        _ffn1_proj_kernel,
        grid=(n // tm,),
        in_specs=[
            pl.BlockSpec((tm, d), lambda i: (i, 0)),
            const((1, d)), const((d, d_ff)), const((d, d_ff)), const((d_ff, d)), const((1, d)),
            const((1, d)), const((d, d_in)),
        ],
        out_specs=[
            pl.BlockSpec((tm, d), lambda i: (i, 0)),
            pl.BlockSpec((1, blocks_per_tile, D_ATTN, MOBA_BLOCK),
                         lambda i: (i // tiles_per_seq, i % tiles_per_seq, 0, 0)),
            pl.BlockSpec((tm, D_ATTN), lambda i: (i, 0)),
            pl.BlockSpec((1, blocks_per_tile, D_ATTN, MOBA_BLOCK),
                         lambda i: (i // tiles_per_seq, i % tiles_per_seq, 0, 0)),
            pl.BlockSpec((tm, d_ssm), lambda i: (i % tiles_per_seq, i // tiles_per_seq)),
            pl.BlockSpec((1, blocks_per_tile, D_ATTN), lambda i: (i, 0, 0)),
        ],
        out_shape=[
            jax.ShapeDtypeStruct((n, d), F32),
            jax.ShapeDtypeStruct((bsz, nb, D_ATTN, MOBA_BLOCK), BF16),
            jax.ShapeDtypeStruct((n, D_ATTN), BF16),
            jax.ShapeDtypeStruct((bsz, nb, D_ATTN, MOBA_BLOCK), BF16),
            jax.ShapeDtypeStruct((seq, bsz * d_ssm), F32),
            jax.ShapeDtypeStruct((n // tm, blocks_per_tile, D_ATTN), F32),
        ],
        scratch_shapes=[pltpu.VMEM((tm, d_ff), BF16)],
        compiler_params=pltpu.CompilerParams(
            dimension_semantics=("arbitrary",), vmem_limit_bytes=VMEM_LIMIT_BYTES),
        name="ffn1_proj",
    )(x2, pre_g, wg, wu, wd, post_g, mix_g, win)


def _moba_kernel(slopes_ref, qt_ref, k_ref, vt_ref, kmean_ref, o_ref):
    hp = pl.program_id(1)
    j = pl.program_id(2)
    blk = MOBA_BLOCK
    nb = k_ref.shape[1]
    qt2 = qt_ref[0, 0]
    chan = lax.broadcasted_iota(jnp.int32, qt2.shape, 0)
    kpos = lax.broadcasted_iota(jnp.int32, (blk, blk), 0)
    qpos = lax.broadcasted_iota(jnp.int32, (blk, blk), 1)
    dist0 = (qpos - kpos).astype(F32)
    nidx = lax.broadcasted_iota(jnp.int32, (nb, blk), 0)
    past = nidx < j
    kmean = kmean_ref[0]

    heads = []
    for hh in range(HEADS_PER_LANE_TILE):
        slope = slopes_ref[hp * HEADS_PER_LANE_TILE + hh]
        lo = hh * HEAD_DIM
        qm = jnp.where((chan >= lo) & (chan < lo + HEAD_DIM), qt2, jnp.zeros_like(qt2))
        gate = jnp.dot(kmean, qm.astype(F32), preferred_element_type=F32,
                       precision=lax.Precision.HIGHEST)
        gate = jnp.where(past, gate, NEG_INF)
        rank = jnp.zeros((nb, blk), jnp.int32)
        for m in range(nb):
            gm = gate[m:m + 1, :]
            beats = (gm > gate) | ((gm == gate) & (m < nidx))
            rank = rank + beats.astype(jnp.int32)
        selected = past & (rank < MOBA_TOPK)
        row_bias = (jnp.where(selected, 0.0, NEG_INF)
                    - slope * ((j - nidx) * blk).astype(F32))
        alibi0 = -slope * dist0
        s = jnp.dot(k_ref[0, j], qm, preferred_element_type=F32) + jnp.where(qpos >= kpos, alibi0, NEG_INF)
        m0 = jnp.max(s, axis=0, keepdims=True)
        p = jnp.exp(s - m0)
        l0 = jnp.sum(p, axis=0, keepdims=True)
        acc0 = jnp.dot(vt_ref[0, j, lo:lo + HEAD_DIM, :], p.astype(BF16), preferred_element_type=F32)
        heads.append((lo, qm, row_bias, alibi0, (m0, l0, acc0)))

    def body(n, carry):
        out = []
        k_n = k_ref[0, n]
        for (lo, qm, row_bias, alibi0, _), (m, l, acc) in zip(heads, carry):
            rb = jnp.sum(jnp.where(nidx == n, row_bias, 0.0), axis=0, keepdims=True)
            s = jnp.dot(k_n, qm, preferred_element_type=F32) + alibi0 + rb
            m_new = jnp.maximum(m, jnp.max(s, axis=0, keepdims=True))
            alpha = jnp.exp(m - m_new)
            p = jnp.exp(s - m_new)
            l = alpha * l + jnp.sum(p, axis=0, keepdims=True)
            acc = alpha * acc + jnp.dot(vt_ref[0, n, lo:lo + HEAD_DIM, :], p.astype(BF16),
                                        preferred_element_type=F32)
            out.append((m_new, l, acc))
        return tuple(out)

    final = lax.fori_loop(0, j, body, tuple(h[4] for h in heads))
    o_t = jnp.concatenate([acc / l for (_, l, acc) in final], axis=0)
    o_ref[0] = o_t.T.astype(o_ref.dtype)


def _moba_attention(slopes, qt, k4, vt, kmean):
    bsz, nb, _, blk = qt.shape
    n_tiles = D_ATTN // LANES
    return pl.pallas_call(
        _moba_kernel,
        grid=(bsz, n_tiles, nb),
        in_specs=[
            pl.BlockSpec(memory_space=pltpu.SMEM),
            pl.BlockSpec((1, 1, LANES, blk), lambda b, hp, j: (b, j, hp, 0)),
            pl.BlockSpec((1, nb, blk, LANES), lambda b, hp, j: (b, 0, 0, hp)),
            pl.BlockSpec((1, nb, LANES, blk), lambda b, hp, j: (b, 0, hp, 0)),
            pl.BlockSpec((1, nb, LANES), lambda b, hp, j: (b, 0, hp)),
        ],
        out_specs=pl.BlockSpec((1, blk, LANES), lambda b, hp, j: (b, j, hp)),
        out_shape=jax.ShapeDtypeStruct((bsz, nb * blk, D_ATTN), BF16),
        compiler_params=pltpu.CompilerParams(
            dimension_semantics=("arbitrary", "arbitrary", "arbitrary")),
        name="moba_attn",
    )(slopes, qt, k4, vt, kmean)


def _s5_kernel(u_ref, wb_ref, are_ref, aim_ref, wc_ref, d_ref, y_ref, xbuf, hstate):
    rows = u_ref.shape[0]
    steps = rows // SUBLANES
    n_tiles = wb_ref.shape[0]
    n_pairs = are_ref.shape[0]

    @pl.when(pl.program_id(0) == 0)
    def _():
        hstate[...] = jnp.zeros_like(hstate)

    for lt in range(n_tiles):
        u_t = u_ref[:, lt * LANES:(lt + 1) * LANES].astype(BF16)
        xbuf[:, lt * STATE_COLS_PER_LANE_TILE:(lt + 1) * STATE_COLS_PER_LANE_TILE] = jnp.dot(
            u_t, wb_ref[lt], preferred_element_type=F32)

    for p0 in range(0, n_pairs, SCAN_PAIRS_PER_LOOP):
        pairs = range(p0, p0 + SCAN_PAIRS_PER_LOOP)
        a_re = [jnp.broadcast_to(are_ref[pi:pi + 1, :], (SUBLANES, LANES)) for pi in pairs]
        a_im = [jnp.broadcast_to(aim_ref[pi:pi + 1, :], (SUBLANES, LANES)) for pi in pairs]
        cols = [pi * 2 * LANES for pi in pairs]
        init = tuple((hstate[:, c:c + LANES], hstate[:, c + LANES:c + 2 * LANES]) for c in cols)

        def step(t, hs):
            r = pl.multiple_of(t * SUBLANES, SUBLANES)
            out = []
            for c, ar, ai, (hr, hi) in zip(cols, a_re, a_im, hs):
                nr = ar * hr - ai * hi + xbuf[pl.ds(r, SUBLANES), c:c + LANES]
                ni = ar * hi + ai * hr + xbuf[pl.ds(r, SUBLANES), c + LANES:c + 2 * LANES]
                xbuf[pl.ds(r, SUBLANES), c:c + LANES] = nr
                xbuf[pl.ds(r, SUBLANES), c + LANES:c + 2 * LANES] = ni
                out.append((nr, ni))
            return tuple(out)

        final = lax.fori_loop(0, steps, step, init, unroll=2)
        for c, (hr, hi) in zip(cols, final):
            hstate[:, c:c + LANES] = hr
            hstate[:, c + LANES:c + 2 * LANES] = hi

    for lt in range(n_tiles):
        h_t = xbuf[:, lt * STATE_COLS_PER_LANE_TILE:(lt + 1) * STATE_COLS_PER_LANE_TILE].astype(BF16)
        lanes = slice(lt * LANES, (lt + 1) * LANES)
        y_ref[:, lanes] = (jnp.dot(h_t, wc_ref[lt], preferred_element_type=F32)
                           + d_ref[:, lanes] * u_ref[:, lanes])


def _s5_weights(lam_re, lam_im, log_dt, b_re, b_im, c_re, c_im):
    g, p = lam_re.shape
    n_tiles = g // GROUPS_PER_LANE_TILE
    dt = jnp.exp(log_dt)[:, None]
    mag = jnp.exp(lam_re * dt)
    ang = lam_im * dt
    a_re = mag * jnp.cos(ang)
    a_im = mag * jnp.sin(ang)
    den = lam_re * lam_re + lam_im * lam_im
    f_re = ((a_re - 1.0) * lam_re + a_im * lam_im) / den
    f_im = (a_im * lam_re - (a_re - 1.0) * lam_im) / den
    bb_re = f_re[..., None] * b_re - f_im[..., None] * b_im
    bb_im = f_re[..., None] * b_im + f_im[..., None] * b_re
    gl = np.arange(GROUPS_PER_LANE_TILE)[:, None, None]
    delta = jnp.asarray(gl == 2 * np.arange(PAIRS_PER_LANE_TILE)[None, :, None] + np.arange(2)[None, None, :], F32)
    bbs = jnp.stack([bb_re, bb_im]).reshape(2, n_tiles, GROUPS_PER_LANE_TILE, p, SSM_GROUP)
    wb = jnp.einsum('rlgpi,gkj->lgikrjp', bbs, delta).reshape(n_tiles, LANES, STATE_COLS_PER_LANE_TILE)
    cs = jnp.stack([c_re, -c_im]).reshape(2, n_tiles, GROUPS_PER_LANE_TILE, SSM_GROUP, p)
    wc = jnp.einsum('rlgop,gkj->lkrjpgo', cs, delta).reshape(n_tiles, STATE_COLS_PER_LANE_TILE, LANES)
    n_pairs = g // 2
    return (wb.astype(BF16), a_re.reshape(n_pairs, 2 * p), a_im.reshape(n_pairs, 2 * p), wc.astype(BF16))


def _s5_scan(us_tm, wb, a_re, a_im, wc, d_skip):
    rows_total, d_ssm = us_tm.shape
    rows = SCAN_STEPS * SUBLANES
    n_tiles = wb.shape[0]
    state_cols = n_tiles * STATE_COLS_PER_LANE_TILE
    const = lambda shape: pl.BlockSpec(shape, lambda c: (0,) * len(shape))
    return pl.pallas_call(
        _s5_kernel,
        grid=(rows_total // rows,),
        in_specs=[
            pl.BlockSpec((rows, d_ssm), lambda c: (c, 0)),
            const(wb.shape), const(a_re.shape), const(a_im.shape), const(wc.shape), const((1, d_ssm)),
        ],
        out_specs=pl.BlockSpec((rows, d_ssm), lambda c: (c, 0)),
        out_shape=jax.ShapeDtypeStruct((rows_total, d_ssm), F32),
        scratch_shapes=[pltpu.VMEM((rows, state_cols), F32), pltpu.VMEM((SUBLANES, state_cols), F32)],
        compiler_params=pltpu.CompilerParams(
            dimension_semantics=("arbitrary",), vmem_limit_bytes=VMEM_LIMIT_BYTES),
        name="s5_scan",
    )(us_tm, wb, a_re, a_im, wc, d_skip)


def _mix_ffn2_kernel(h_ref, attn_ref, y_ref, wglu, bglu, attn_g, ssm_g, wout, mixpost_g,
                     pre_g, wg, wu, wd, post_g, o_ref, act_ref):
    y = jax.nn.gelu(y_ref[...], approximate=True)
    z = y * jax.nn.sigmoid(jnp.dot(y.astype(BF16), wglu[...], preferred_element_type=F32) + bglu[...])
    zs = _rms(z, ssm_g[...]).astype(BF16)
    at = _rms(attn_ref[...].astype(F32), attn_g[...]).astype(BF16)
    mixed = (jnp.dot(at, wout[0:D_ATTN, :], preferred_element_type=F32)
             + jnp.dot(zs, wout[D_ATTN:, :], preferred_element_type=F32))
    h = h_ref[...] + _rms(mixed, mixpost_g[...])
    f = _swiglu(_rms(h, pre_g[...]).astype(BF16), wg, wu, wd, act_ref)
    o_ref[...] = h + 0.5 * _rms(f, post_g[...])


def _mix_ffn2(h2, attn2, y_tm, wglu, bglu, attn_g, ssm_g, wout, mixpost_g, pre_g, wg, wu, wd, post_g, seq):
    n, d = h2.shape
    d_ff = wg.shape[1]
    d_ssm = wglu.shape[0]
    tm = ROW_TILE
    tiles_per_seq = seq // tm
    const = lambda shape: pl.BlockSpec(shape, lambda i: (0,) * len(shape), pipeline_mode=pl.Buffered(1))
    return pl.pallas_call(
        _mix_ffn2_kernel,
        grid=(n // tm,),
        in_specs=[
            pl.BlockSpec((tm, d), lambda i: (i, 0)),
            pl.BlockSpec((tm, D_ATTN), lambda i: (i, 0)),
            pl.BlockSpec((tm, d_ssm), lambda i: (i % tiles_per_seq, i // tiles_per_seq)),
            const((d_ssm, d_ssm)), const((1, d_ssm)), const((1, D_ATTN)), const((1, d_ssm)),
            const((D_ATTN + d_ssm, d)), const((1, d)),
            const((1, d)), const((d, d_ff)), const((d, d_ff)), const((d_ff, d)), const((1, d)),
        ],
        out_specs=pl.BlockSpec((tm, d), lambda i: (i, 0)),
        out_shape=jax.ShapeDtypeStruct((n, d), F32),
        scratch_shapes=[pltpu.VMEM((tm, d_ff), BF16)],
        compiler_params=pltpu.CompilerParams(
            dimension_semantics=("arbitrary",), vmem_limit_bytes=VMEM_LIMIT_BYTES),
        name="mix_ffn2",
    )(h2, attn2, y_tm, wglu, bglu, attn_g, ssm_g, wout, mixpost_g, pre_g, wg, wu, wd, post_g)


def kernel(x, ffn1_pre_g, ffn1_w_gate, ffn1_w_up, ffn1_w_down, ffn1_post_g, mix_pre_g, w_in, lam_re, lam_im, log_dt, b_re, b_im, c_re, c_im, d_skip, w_glu, b_glu, attn_out_g, ssm_out_g, w_out, mix_post_g, ffn2_pre_g, ffn2_w_gate, ffn2_w_up, ffn2_w_down, ffn2_post_g):
    bsz, seq, d = x.shape
    depth = w_in.shape[0]
    d_ssm = w_glu.shape[1]
    assert w_in.shape[2] == 3 * D_ATTN + d_ssm and d_ssm == lam_re.shape[1] * SSM_GROUP
    assert seq % ROW_TILE == 0 and ROW_TILE % MOBA_BLOCK == 0 and seq % SCAN_STEPS == 0
    assert bsz == SUBLANES and lam_re.shape[2] == SSM_STATE
    nb = seq // MOBA_BLOCK
    slopes = jnp.asarray(2.0 ** (-8.0 * np.arange(1, N_HEADS + 1) / N_HEADS), dtype=F32)
    row = lambda g: g.reshape(1, -1).astype(F32)
    bf = lambda w: w.astype(BF16)

    h2 = x.reshape(bsz * seq, d)
    for l in range(depth):
        h2, qt, k2, vt, us_tm, kmean = _ffn1_proj(
            h2, row(ffn1_pre_g[l]), bf(ffn1_w_gate[l]), bf(ffn1_w_up[l]), bf(ffn1_w_down[l]),
            row(ffn1_post_g[l]), row(mix_pre_g[l]), bf(w_in[l]), bsz, seq)
        attn = _moba_attention(slopes, qt, k2.reshape(bsz, nb, MOBA_BLOCK, D_ATTN), vt,
                               kmean.reshape(bsz, nb, D_ATTN))
        wb, a_re, a_im, wc = _s5_weights(lam_re[l], lam_im[l], log_dt[l], b_re[l], b_im[l], c_re[l], c_im[l])
        y_tm = _s5_scan(us_tm.reshape(seq * bsz, d_ssm), wb, a_re, a_im, wc, row(d_skip[l]))
        h2 = _mix_ffn2(
            h2, attn.reshape(bsz * seq, D_ATTN), y_tm.reshape(seq, bsz * d_ssm),
            bf(w_glu[l]), row(b_glu[l]), row(attn_out_g[l]), row(ssm_out_g[l]), bf(w_out[l]),
            row(mix_post_g[l]), row(ffn2_pre_g[l]), bf(ffn2_w_gate[l]), bf(ffn2_w_up[l]),
            bf(ffn2_w_down[l]), row(ffn2_post_g[l]), seq)
    return h2.reshape(bsz, seq, d)
```

```python
import functools
import math

import numpy as np
import jax
import jax.numpy as jnp
from jax import lax
from jax.experimental import pallas as pl
from jax.experimental.pallas import tpu as pltpu

F32 = jnp.float32
BF16 = jnp.bfloat16

N_HEADS = 8
HEAD_DIM = 64
D_ATTN = N_HEADS * HEAD_DIM
MOBA_BLOCK = 256
MOBA_TOPK = 3
SSM_GROUP = 16
SSM_STATE = 64
RMS_EPS = 1e-6
NEG_INF = -1e30

LANES = 128
SUBLANES = 8
HEADS_PER_LANE_TILE = LANES // HEAD_DIM
GROUPS_PER_LANE_TILE = LANES // SSM_GROUP
PAIRS_PER_LANE_TILE = GROUPS_PER_LANE_TILE // 2
STATE_COLS_PER_LANE_TILE = GROUPS_PER_LANE_TILE * 2 * SSM_STATE

ROW_TILE = 512
SCAN_STEPS = 64
SCAN_PAIRS_PER_LOOP = 4
SCORE_LOOKAHEAD = 2
VMEM_LIMIT_BYTES = 56 * 1024 * 1024
ATTN_VMEM_LIMIT_BYTES = 32 * 1024 * 1024
LOG2E = math.log2(math.e)


def _rms(x, g):
    return x * lax.rsqrt(jnp.mean(x * x, axis=-1, keepdims=True) + RMS_EPS) * g


def _ff_chunks(d_ff, width=768):
    return [(s, min(width, d_ff - s)) for s in range(0, d_ff, width)]


def _swiglu(xn, wg_ref, wu_ref, wd_ref, act_ref):
    for s, w in _ff_chunks(wg_ref.shape[1]):
        g = jnp.dot(xn, wg_ref[:, s:s + w], preferred_element_type=F32)
        u = jnp.dot(xn, wu_ref[:, s:s + w], preferred_element_type=F32)
        act_ref[:, s:s + w] = (g * jax.nn.sigmoid(g) * u).astype(BF16)
    return jnp.dot(act_ref[...], wd_ref[...], preferred_element_type=F32)


def _ffn1_proj_kernel(x_ref, pre_g, wg, wu, wd, post_g, mix_g, win,
                      h_ref, qt_ref, k_ref, vt_ref, us_ref, kmean_ref, act_ref):
    x = x_ref[...]
    f = _swiglu(_rms(x, pre_g[...]).astype(BF16), wg, wu, wd, act_ref)
    h = x + 0.5 * _rms(f, post_g[...])
    h_ref[...] = h
    u = _rms(h, mix_g[...]).astype(BF16)
    q = jnp.dot(u, win[:, 0:D_ATTN], preferred_element_type=F32) * (LOG2E / math.sqrt(HEAD_DIM))
    k = jnp.dot(u, win[:, D_ATTN:2 * D_ATTN], preferred_element_type=F32)
    v = jnp.dot(u, win[:, 2 * D_ATTN:3 * D_ATTN], preferred_element_type=F32)
    us_ref[...] = jnp.dot(u, win[:, 3 * D_ATTN:], preferred_element_type=F32)
    k_ref[...] = k.astype(BF16)
    for blk in range(x.shape[0] // MOBA_BLOCK):
        rows = slice(blk * MOBA_BLOCK, (blk + 1) * MOBA_BLOCK)
        qt_ref[0, blk] = q[rows].T.astype(BF16)
        vt_ref[0, blk] = v[rows].T.astype(BF16)
        kmean_ref[0, blk:blk + 1, :] = jnp.mean(k[rows], axis=0, keepdims=True)


def _ffn1_proj(x2, pre_g, wg, wu, wd, post_g, mix_g, win, bsz, seq):
    n, d = x2.shape
    d_ff = wg.shape[1]
    d_in = win.shape[1]
    d_ssm = d_in - 3 * D_ATTN
    tm = ROW_TILE
    tiles_per_seq = seq // tm
    blocks_per_tile = tm // MOBA_BLOCK
    nb = seq // MOBA_BLOCK
    const = lambda shape: pl.BlockSpec(shape, lambda i: (0,) * len(shape), pipeline_mode=pl.Buffered(1))
    return pl.pallas_call(
        _ffn1_proj_kernel,
        grid=(n // tm,),
        in_specs=[
            pl.BlockSpec((tm, d), lambda i: (i, 0)),
            const((1, d)), const((d, d_ff)), const((d, d_ff)), const((d_ff, d)), const((1, d)),
            const((1, d)), const((d, d_in)),
        ],
        out_specs=[
            pl.BlockSpec((tm, d), lambda i: (i, 0)),
            pl.BlockSpec((1, blocks_per_tile, D_ATTN, MOBA_BLOCK),
                         lambda i: (i // tiles_per_seq, i % tiles_per_seq, 0, 0)),
            pl.BlockSpec((tm, D_ATTN), lambda i: (i, 0)),
            pl.BlockSpec((1, blocks_per_tile, D_ATTN, MOBA_BLOCK),
                         lambda i: (i // tiles_per_seq, i % tiles_per_seq, 0, 0)),
            pl.BlockSpec((tm, d_ssm), lambda i: (i % tiles_per_seq, i // tiles_per_seq)),
            pl.BlockSpec((1, blocks_per_tile, D_ATTN), lambda i: (i, 0, 0)),
        ],
        out_shape=[
            jax.ShapeDtypeStruct((n, d), F32),
            jax.ShapeDtypeStruct((bsz, nb, D_ATTN, MOBA_BLOCK), BF16),
            jax.ShapeDtypeStruct((n, D_ATTN), BF16),
            jax.ShapeDtypeStruct((bsz, nb, D_ATTN, MOBA_BLOCK), BF16),
            jax.ShapeDtypeStruct((seq, bsz * d_ssm), F32),
            jax.ShapeDtypeStruct((n // tm, blocks_per_tile, D_ATTN), F32),
        ],
        scratch_shapes=[pltpu.VMEM((tm, d_ff), BF16)],
        compiler_params=pltpu.CompilerParams(
            dimension_semantics=("arbitrary",), vmem_limit_bytes=VMEM_LIMIT_BYTES),
        name="ffn1_proj",
    )(x2, pre_g, wg, wu, wd, post_g, mix_g, win)


def _moba_kernel(slopes_ref, qt_ref, k_ref, vt_ref, kmean_ref, o_ref,
                 qm_ref, bias_ref, rowbias_ref, s_ref, m_ref, l_ref, acc_ref):
    j = pl.program_id(1)
    blk = MOBA_BLOCK
    nb = k_ref.shape[1]
    kpos = lax.broadcasted_iota(jnp.int32, (blk, blk), 0)
    qpos = lax.broadcasted_iota(jnp.int32, (blk, blk), 1)
    nidx = lax.broadcasted_iota(jnp.int32, (nb, blk), 0)
    qcol = lax.broadcasted_iota(jnp.int32, (nb, blk), 1)
    chan = lax.broadcasted_iota(jnp.int32, (LANES, blk), 0)
    past = nidx < j

    @pl.when((pl.program_id(0) == 0) & (j == 0))
    def _():
        for h in range(N_HEADS):
            slope = slopes_ref[h]
            bias_ref[0, h] = slope * kpos.astype(F32)
            bias_ref[1, h] = jnp.where(qpos >= kpos, slope * (kpos - qpos).astype(F32), NEG_INF)

    def lane_tile(h):
        t = h // HEADS_PER_LANE_TILE
        return slice(t * LANES, (t + 1) * LANES)

    for h in range(N_HEADS):
        slope = slopes_ref[h]
        lanes = lane_tile(h)
        lo = (h % HEADS_PER_LANE_TILE) * HEAD_DIM
        qt2 = qt_ref[0, 0, lanes, :]
        qm = jnp.where((chan >= lo) & (chan < lo + HEAD_DIM), qt2, jnp.zeros_like(qt2))
        qm_ref[h] = qm
        gate = jnp.dot(kmean_ref[0, :, lanes], qm.astype(F32), preferred_element_type=F32,
                       precision=lax.Precision.HIGHEST)
        gate = jnp.where(past, gate, NEG_INF)
        rank = jnp.zeros((nb, blk), jnp.int32)
        for m in range(nb):
            gm = gate[m:m + 1, :]
            beats = (gm > gate) | ((gm == gate) & (m < nidx))
            rank = rank + beats.astype(jnp.int32)
        selected = past & (rank < MOBA_TOPK)
        rowbias_ref[h] = jnp.where(
            nidx == j, 0.0,
            jnp.where(selected, 0.0, NEG_INF) - slope * ((j - nidx) * blk + qcol).astype(F32))
        m_ref[h] = jnp.full((1, blk), NEG_INF, F32)
        l_ref[h] = jnp.zeros((1, blk), F32)
        acc_ref[h] = jnp.zeros((HEAD_DIM, blk), F32)

    def scores(n, h):
        s_ref[h] = jnp.dot(k_ref[0, n, :, lane_tile(h)], qm_ref[h], preferred_element_type=F32)

    def accumulate(n, own, h):
        t = s_ref[h] + bias_ref[own, h]
        rb = jnp.sum(jnp.where(nidx == n, rowbias_ref[h], 0.0), axis=0, keepdims=True)
        m = m_ref[h]
        m_new = jnp.maximum(m, jnp.max(t, axis=0, keepdims=True) + rb)
        p = jnp.exp2(t - (m_new - rb))
        alpha = jnp.exp2(m - m_new)
        m_ref[h] = m_new
        l_ref[h] = alpha * l_ref[h] + jnp.sum(p, axis=0, keepdims=True)
        pv = jnp.dot(vt_ref[0, n, h * HEAD_DIM:(h + 1) * HEAD_DIM, :], p.astype(BF16),
                     preferred_element_type=F32)
        acc_ref[h] = alpha * acc_ref[h] + pv

    def body(i, carry):
        n = j - i
        own = (i == 0).astype(jnp.int32)
        for h in range(SCORE_LOOKAHEAD):
            scores(n, h)
        for h in range(N_HEADS):
            if h + SCORE_LOOKAHEAD < N_HEADS:
                scores(n, h + SCORE_LOOKAHEAD)
            accumulate(n, own, h)
        return carry

    lax.fori_loop(0, j + 1, body, 0)
    for t in range(N_HEADS // HEADS_PER_LANE_TILE):
        o_t = jnp.concatenate([acc_ref[h] / l_ref[h] for h in range(t * HEADS_PER_LANE_TILE,
                                                                     (t + 1) * HEADS_PER_LANE_TILE)], axis=0)
        o_ref[0, :, t * LANES:(t + 1) * LANES] = o_t.T.astype(o_ref.dtype)


def _moba_attention(slopes, qt, k4, vt, kmean):
    bsz, nb, _, blk = qt.shape
    return pl.pallas_call(
        _moba_kernel,
        grid=(bsz, nb),
        in_specs=[
            pl.BlockSpec(memory_space=pltpu.SMEM),
            pl.BlockSpec((1, 1, D_ATTN, blk), lambda b, j: (b, j, 0, 0)),
            pl.BlockSpec((1, nb, blk, D_ATTN), lambda b, j: (b, 0, 0, 0)),
            pl.BlockSpec((1, nb, D_ATTN, blk), lambda b, j: (b, 0, 0, 0)),
            pl.BlockSpec((1, nb, D_ATTN), lambda b, j: (b, 0, 0)),
        ],
        out_specs=pl.BlockSpec((1, blk, D_ATTN), lambda b, j: (b, j, 0)),
        out_shape=jax.ShapeDtypeStruct((bsz, nb * blk, D_ATTN), BF16),
        scratch_shapes=[
            pltpu.VMEM((N_HEADS, LANES, blk), BF16),
            pltpu.VMEM((2, N_HEADS, blk, blk), F32),
            pltpu.VMEM((N_HEADS, nb, blk), F32),
            pltpu.VMEM((N_HEADS, blk, blk), F32),
            pltpu.VMEM((N_HEADS, 1, blk), F32),
            pltpu.VMEM((N_HEADS, 1, blk), F32),
            pltpu.VMEM((N_HEADS, HEAD_DIM, blk), F32),
        ],
        compiler_params=pltpu.CompilerParams(
            dimension_semantics=("arbitrary", "arbitrary"), vmem_limit_bytes=ATTN_VMEM_LIMIT_BYTES),
        name="moba_attn",
    )(slopes, qt, k4, vt, kmean)


def _s5_kernel(u_ref, wb_ref, are_ref, aim_ref, wc_ref, d_ref, y_ref, xbuf, hstate):
    rows = u_ref.shape[0]
    steps = rows // SUBLANES
    n_tiles = wb_ref.shape[0]
    n_pairs = are_ref.shape[0]

    @pl.when(pl.program_id(0) == 0)
    def _():
        hstate[...] = jnp.zeros_like(hstate)

    for lt in range(n_tiles):
        u_t = u_ref[:, lt * LANES:(lt + 1) * LANES].astype(BF16)
        xbuf[:, lt * STATE_COLS_PER_LANE_TILE:(lt + 1) * STATE_COLS_PER_LANE_TILE] = jnp.dot(
            u_t, wb_ref[lt], preferred_element_type=F32)

    for p0 in range(0, n_pairs, SCAN_PAIRS_PER_LOOP):
        pairs = range(p0, p0 + SCAN_PAIRS_PER_LOOP)
        a_re = [jnp.broadcast_to(are_ref[pi:pi + 1, :], (SUBLANES, LANES)) for pi in pairs]
        a_im = [jnp.broadcast_to(aim_ref[pi:pi + 1, :], (SUBLANES, LANES)) for pi in pairs]
        cols = [pi * 2 * LANES for pi in pairs]
        init = tuple((hstate[:, c:c + LANES], hstate[:, c + LANES:c + 2 * LANES]) for c in cols)

        def step(t, hs):
            r = pl.multiple_of(t * SUBLANES, SUBLANES)
            out = []
            for c, ar, ai, (hr, hi) in zip(cols, a_re, a_im, hs):
                nr = ar * hr - ai * hi + xbuf[pl.ds(r, SUBLANES), c:c + LANES]
                ni = ar * hi + ai * hr + xbuf[pl.ds(r, SUBLANES), c + LANES:c + 2 * LANES]
                xbuf[pl.ds(r, SUBLANES), c:c + LANES] = nr
                xbuf[pl.ds(r, SUBLANES), c + LANES:c + 2 * LANES] = ni
                out.append((nr, ni))
            return tuple(out)

        final = lax.fori_loop(0, steps, step, init, unroll=2)
        for c, (hr, hi) in zip(cols, final):
            hstate[:, c:c + LANES] = hr
            hstate[:, c + LANES:c + 2 * LANES] = hi

    for lt in range(n_tiles):
        h_t = xbuf[:, lt * STATE_COLS_PER_LANE_TILE:(lt + 1) * STATE_COLS_PER_LANE_TILE].astype(BF16)
        lanes = slice(lt * LANES, (lt + 1) * LANES)
        y_ref[:, lanes] = (jnp.dot(h_t, wc_ref[lt], preferred_element_type=F32)
                           + d_ref[:, lanes] * u_ref[:, lanes])


def _s5_weights(lam_re, lam_im, log_dt, b_re, b_im, c_re, c_im):
    g, p = lam_re.shape
    n_tiles = g // GROUPS_PER_LANE_TILE
    dt = jnp.exp(log_dt)[:, None]
    mag = jnp.exp(lam_re * dt)
    ang = lam_im * dt
    a_re = mag * jnp.cos(ang)
    a_im = mag * jnp.sin(ang)
    den = lam_re * lam_re + lam_im * lam_im
    f_re = ((a_re - 1.0) * lam_re + a_im * lam_im) / den
    f_im = (a_im * lam_re - (a_re - 1.0) * lam_im) / den
    bb_re = f_re[..., None] * b_re - f_im[..., None] * b_im
    bb_im = f_re[..., None] * b_im + f_im[..., None] * b_re
    gl = np.arange(GROUPS_PER_LANE_TILE)[:, None, None]
    delta = jnp.asarray(gl == 2 * np.arange(PAIRS_PER_LANE_TILE)[None, :, None] + np.arange(2)[None, None, :], F32)
    bbs = jnp.stack([bb_re, bb_im]).reshape(2, n_tiles, GROUPS_PER_LANE_TILE, p, SSM_GROUP)
    wb = jnp.einsum('rlgpi,gkj->lgikrjp', bbs, delta).reshape(n_tiles, LANES, STATE_COLS_PER_LANE_TILE)
    cs = jnp.stack([c_re, -c_im]).reshape(2, n_tiles, GROUPS_PER_LANE_TILE, SSM_GROUP, p)
    wc = jnp.einsum('rlgop,gkj->lkrjpgo', cs, delta).reshape(n_tiles, STATE_COLS_PER_LANE_TILE, LANES)
    n_pairs = g // 2
    return (wb.astype(BF16), a_re.reshape(n_pairs, 2 * p), a_im.reshape(n_pairs, 2 * p), wc.astype(BF16))


def _s5_scan(us_tm, wb, a_re, a_im, wc, d_skip):
    rows_total, d_ssm = us_tm.shape
    rows = SCAN_STEPS * SUBLANES
    n_tiles = wb.shape[0]
    state_cols = n_tiles * STATE_COLS_PER_LANE_TILE
    const = lambda shape: pl.BlockSpec(shape, lambda c: (0,) * len(shape))
    return pl.pallas_call(
        _s5_kernel,
        grid=(rows_total // rows,),
        in_specs=[
            pl.BlockSpec((rows, d_ssm), lambda c: (c, 0)),
            const(wb.shape), const(a_re.shape), const(a_im.shape), const(wc.shape), const((1, d_ssm)),
        ],
        out_specs=pl.BlockSpec((rows, d_ssm), lambda c: (c, 0)),
        out_shape=jax.ShapeDtypeStruct((rows_total, d_ssm), F32),
        scratch_shapes=[pltpu.VMEM((rows, state_cols), F32), pltpu.VMEM((SUBLANES, state_cols), F32)],
        compiler_params=pltpu.CompilerParams(
            dimension_semantics=("arbitrary",), vmem_limit_bytes=VMEM_LIMIT_BYTES),
        name="s5_scan",
    )(us_tm, wb, a_re, a_im, wc, d_skip)


def _mix_ffn2_kernel(h_ref, attn_ref, y_ref, wglu, bglu, attn_g, ssm_g, wout, mixpost_g,
                     pre_g, wg, wu, wd, post_g, o_ref, act_ref):
    y = jax.nn.gelu(y_ref[...], approximate=True)
    z = y * jax.nn.sigmoid(jnp.dot(y.astype(BF16), wglu[...], preferred_element_type=F32) + bglu[...])
    zs = _rms(z, ssm_g[...]).astype(BF16)
    at = _rms(attn_ref[...].astype(F32), attn_g[...]).astype(BF16)
    mixed = (jnp.dot(at, wout[0:D_ATTN, :], preferred_element_type=F32)
             + jnp.dot(zs, wout[D_ATTN:, :], preferred_element_type=F32))
    h = h_ref[...] + _rms(mixed, mixpost_g[...])
    f = _swiglu(_rms(h, pre_g[...]).astype(BF16), wg, wu, wd, act_ref)
    o_ref[...] = h + 0.5 * _rms(f, post_g[...])


def _mix_ffn2(h2, attn2, y_tm, wglu, bglu, attn_g, ssm_g, wout, mixpost_g, pre_g, wg, wu, wd, post_g, seq):
    n, d = h2.shape
    d_ff = wg.shape[1]
    d_ssm = wglu.shape[0]
    tm = ROW_TILE
    tiles_per_seq = seq // tm
    const = lambda shape: pl.BlockSpec(shape, lambda i: (0,) * len(shape), pipeline_mode=pl.Buffered(1))
    return pl.pallas_call(
        _mix_ffn2_kernel,
        grid=(n // tm,),
        in_specs=[
            pl.BlockSpec((tm, d), lambda i: (i, 0)),
            pl.BlockSpec((tm, D_ATTN), lambda i: (i, 0)),
            pl.BlockSpec((tm, d_ssm), lambda i: (i % tiles_per_seq, i // tiles_per_seq)),
            const((d_ssm, d_ssm)), const((1, d_ssm)), const((1, D_ATTN)), const((1, d_ssm)),
            const((D_ATTN + d_ssm, d)), const((1, d)),
            const((1, d)), const((d, d_ff)), const((d, d_ff)), const((d_ff, d)), const((1, d)),
        ],
        out_specs=pl.BlockSpec((tm, d), lambda i: (i, 0)),
        out_shape=jax.ShapeDtypeStruct((n, d), F32),
        scratch_shapes=[pltpu.VMEM((tm, d_ff), BF16)],
        compiler_params=pltpu.CompilerParams(
            dimension_semantics=("arbitrary",), vmem_limit_bytes=VMEM_LIMIT_BYTES),
        name="mix_ffn2",
    )(h2, attn2, y_tm, wglu, bglu, attn_g, ssm_g, wout, mixpost_g, pre_g, wg, wu, wd, post_g)


def kernel(x, ffn1_pre_g, ffn1_w_gate, ffn1_w_up, ffn1_w_down, ffn1_post_g, mix_pre_g, w_in, lam_re, lam_im, log_dt, b_re, b_im, c_re, c_im, d_skip, w_glu, b_glu, attn_out_g, ssm_out_g, w_out, mix_post_g, ffn2_pre_g, ffn2_w_gate, ffn2_w_up, ffn2_w_down, ffn2_post_g):
    bsz, seq, d = x.shape
    depth = w_in.shape[0]
    d_ssm = w_glu.shape[1]
    assert w_in.shape[2] == 3 * D_ATTN + d_ssm and d_ssm == lam_re.shape[1] * SSM_GROUP
    assert seq % ROW_TILE == 0 and ROW_TILE % MOBA_BLOCK == 0 and seq % SCAN_STEPS == 0
    assert bsz == SUBLANES and lam_re.shape[2] == SSM_STATE
    nb = seq // MOBA_BLOCK
    slopes = jnp.asarray(LOG2E * 2.0 ** (-8.0 * np.arange(1, N_HEADS + 1) / N_HEADS), dtype=F32)
    row = lambda g: g.reshape(1, -1).astype(F32)
    bf = lambda w: w.astype(BF16)

    h2 = x.reshape(bsz * seq, d)
    for l in range(depth):
        h2, qt, k2, vt, us_tm, kmean = _ffn1_proj(
            h2, row(ffn1_pre_g[l]), bf(ffn1_w_gate[l]), bf(ffn1_w_up[l]), bf(ffn1_w_down[l]),
            row(ffn1_post_g[l]), row(mix_pre_g[l]), bf(w_in[l]), bsz, seq)
        attn = _moba_attention(slopes, qt, k2.reshape(bsz, nb, MOBA_BLOCK, D_ATTN), vt,
                               kmean.reshape(bsz, nb, D_ATTN))
        wb, a_re, a_im, wc = _s5_weights(lam_re[l], lam_im[l], log_dt[l], b_re[l], b_im[l], c_re[l], c_im[l])
        y_tm = _s5_scan(us_tm.reshape(seq * bsz, d_ssm), wb, a_re, a_im, wc, row(d_skip[l]))
        h2 = _mix_ffn2(
            h2, attn.reshape(bsz * seq, D_ATTN), y_tm.reshape(seq, bsz * d_ssm),
            bf(w_glu[l]), row(b_glu[l]), row(attn_out_g[l]), row(ssm_out_g[l]), bf(w_out[l]),
            row(mix_post_g[l]), row(ffn2_pre_g[l]), bf(ffn2_w_gate[l]), bf(ffn2_w_up[l]),
            bf(ffn2_w_down[l]), row(ffn2_post_g[l]), seq)
    return h2.reshape(bsz, seq, d)
```

```python
import functools
import math

import numpy as np
import jax
import jax.numpy as jnp
from jax import lax
from jax.experimental import pallas as pl
from jax.experimental.pallas import tpu as pltpu

F32 = jnp.float32
BF16 = jnp.bfloat16

N_HEADS = 8
HEAD_DIM = 64
D_ATTN = N_HEADS * HEAD_DIM
MOBA_BLOCK = 256
MOBA_TOPK = 3
SSM_GROUP = 16
SSM_STATE = 64
RMS_EPS = 1e-6
NEG_INF = -1e30

LANES = 128
SUBLANES = 8
BF16_SUBLANES = 16
HEADS_PER_LANE_TILE = LANES // HEAD_DIM
GROUPS_PER_LANE_TILE = LANES // SSM_GROUP
PAIRS_PER_LANE_TILE = GROUPS_PER_LANE_TILE // 2
STATE_COLS_PER_LANE_TILE = GROUPS_PER_LANE_TILE * 2 * SSM_STATE

ROW_TILE = 512
SCAN_STEPS = 64
SCAN_PAIRS_PER_LOOP = 4
SCORE_LOOKAHEAD = 4
VMEM_LIMIT_BYTES = 56 * 1024 * 1024
ATTN_VMEM_LIMIT_BYTES = 32 * 1024 * 1024
LOG2E = math.log2(math.e)


def _rms(x, g):
    return x * lax.rsqrt(jnp.mean(x * x, axis=-1, keepdims=True) + RMS_EPS) * g


def _ff_chunks(d_ff, width=768):
    return [(s, min(width, d_ff - s)) for s in range(0, d_ff, width)]


def _swiglu(xn, wg_ref, wu_ref, wd_ref, act_ref):
    for s, w in _ff_chunks(wg_ref.shape[1]):
        g = jnp.dot(xn, wg_ref[:, s:s + w], preferred_element_type=F32)
        u = jnp.dot(xn, wu_ref[:, s:s + w], preferred_element_type=F32)
        act_ref[:, s:s + w] = (g * jax.nn.sigmoid(g) * u).astype(BF16)
    return jnp.dot(act_ref[...], wd_ref[...], preferred_element_type=F32)


def _ffn1_proj_kernel(x_ref, pre_g, wg, wu, wd, post_g, mix_g, win,
                      h_ref, qt_ref, k_ref, vt_ref, us_ref, kmean_ref, act_ref):
    x = x_ref[...]
    f = _swiglu(_rms(x, pre_g[...]).astype(BF16), wg, wu, wd, act_ref)
    h = x + 0.5 * _rms(f, post_g[...])
    h_ref[...] = h
    u = _rms(h, mix_g[...]).astype(BF16)
    q = jnp.dot(u, win[:, 0:D_ATTN], preferred_element_type=F32) * (LOG2E / math.sqrt(HEAD_DIM))
    k = jnp.dot(u, win[:, D_ATTN:2 * D_ATTN], preferred_element_type=F32)
    v = jnp.dot(u, win[:, 2 * D_ATTN:3 * D_ATTN], preferred_element_type=F32)
    us_ref[...] = jnp.dot(u, win[:, 3 * D_ATTN:], preferred_element_type=F32)
    k_ref[...] = k.astype(BF16)
    for blk in range(x.shape[0] // MOBA_BLOCK):
        rows = slice(blk * MOBA_BLOCK, (blk + 1) * MOBA_BLOCK)
        qt_ref[0, blk] = q[rows].T.astype(BF16)
        vt_ref[0, blk] = v[rows].T.astype(BF16)
        kmean_ref[0, blk:blk + 1, :] = jnp.mean(k[rows], axis=0, keepdims=True)


def _ffn1_proj(x2, pre_g, wg, wu, wd, post_g, mix_g, win, bsz, seq):
    n, d = x2.shape
    d_ff = wg.shape[1]
    d_in = win.shape[1]
    d_ssm = d_in - 3 * D_ATTN
    tm = ROW_TILE
    tiles_per_seq = seq // tm
    blocks_per_tile = tm // MOBA_BLOCK
    nb = seq // MOBA_BLOCK
    const = lambda shape: pl.BlockSpec(shape, lambda i: (0,) * len(shape), pipeline_mode=pl.Buffered(1))
    return pl.pallas_call(
        _ffn1_proj_kernel,
        grid=(n // tm,),
        in_specs=[
            pl.BlockSpec((tm, d), lambda i: (i, 0)),
            const((1, d)), const((d, d_ff)), const((d, d_ff)), const((d_ff, d)), const((1, d)),
            const((1, d)), const((d, d_in)),
        ],
        out_specs=[
            pl.BlockSpec((tm, d), lambda i: (i, 0)),
            pl.BlockSpec((1, blocks_per_tile, D_ATTN, MOBA_BLOCK),
                         lambda i: (i // tiles_per_seq, i % tiles_per_seq, 0, 0)),
            pl.BlockSpec((tm, D_ATTN), lambda i: (i, 0)),
            pl.BlockSpec((1, blocks_per_tile, D_ATTN, MOBA_BLOCK),
                         lambda i: (i // tiles_per_seq, i % tiles_per_seq, 0, 0)),
            pl.BlockSpec((tm, d_ssm), lambda i: (i % tiles_per_seq, i // tiles_per_seq)),
            pl.BlockSpec((1, blocks_per_tile, D_ATTN), lambda i: (i, 0, 0)),
        ],
        out_shape=[
            jax.ShapeDtypeStruct((n, d), F32),
            jax.ShapeDtypeStruct((bsz, nb, D_ATTN, MOBA_BLOCK), BF16),
            jax.ShapeDtypeStruct((n, D_ATTN), BF16),
            jax.ShapeDtypeStruct((bsz, nb, D_ATTN, MOBA_BLOCK), BF16),
            jax.ShapeDtypeStruct((seq, bsz * d_ssm), F32),
            jax.ShapeDtypeStruct((n // tm, blocks_per_tile, D_ATTN), F32),
        ],
        scratch_shapes=[pltpu.VMEM((tm, d_ff), BF16)],
        compiler_params=pltpu.CompilerParams(
            dimension_semantics=("arbitrary",), vmem_limit_bytes=VMEM_LIMIT_BYTES),
        name="ffn1_proj",
    )(x2, pre_g, wg, wu, wd, post_g, mix_g, win)


def _tree_max_rows(t):
    while t.shape[0] > SUBLANES:
        half = t.shape[0] // 2
        t = jnp.maximum(t[:half], t[half:])
    return jnp.max(t, axis=0, keepdims=True)


def _moba_kernel(slopes_ref, qt_ref, k_ref, vt_ref, kmean_ref, o_ref,
                 qm_ref, bias_ref, rowbias_ref, s_ref, m_ref, l_ref, acc_ref):
    j = pl.program_id(1)
    blk = MOBA_BLOCK
    nb = k_ref.shape[1]
    kpos = lax.broadcasted_iota(jnp.int32, (blk, blk), 0)
    qpos = lax.broadcasted_iota(jnp.int32, (blk, blk), 1)
    nidx = lax.broadcasted_iota(jnp.int32, (nb, blk), 0)
    qcol = lax.broadcasted_iota(jnp.int32, (nb, blk), 1)
    chan = lax.broadcasted_iota(jnp.int32, (LANES, blk), 0)
    past = nidx < j
    ones = jnp.ones((BF16_SUBLANES, blk), BF16)

    @pl.when((pl.program_id(0) == 0) & (j == 0))
    def _():
        for h in range(N_HEADS):
            slope = slopes_ref[h]
            bias_ref[0, h] = slope * kpos.astype(F32)
            bias_ref[1, h] = jnp.where(qpos >= kpos, slope * (kpos - qpos).astype(F32), NEG_INF)

    def lane_tile(h):
        t = h // HEADS_PER_LANE_TILE
        return slice(t * LANES, (t + 1) * LANES)

    for h in range(N_HEADS):
        slope = slopes_ref[h]
        lanes = lane_tile(h)
        lo = (h % HEADS_PER_LANE_TILE) * HEAD_DIM
        qt2 = qt_ref[0, 0, lanes, :]
        qm = jnp.where((chan >= lo) & (chan < lo + HEAD_DIM), qt2, jnp.zeros_like(qt2))
        qm_ref[h] = qm
        km = kmean_ref[0, :, lanes]
        km_hi = km.astype(BF16).astype(F32)
        km_mid = (km - km_hi).astype(BF16).astype(F32)
        km_lo = km - km_hi - km_mid
        gate3 = jnp.dot(jnp.concatenate([km_hi, km_mid, km_lo], axis=0).astype(BF16), qm,
                        preferred_element_type=F32)
        gate = gate3[0:nb] + gate3[nb:2 * nb] + gate3[2 * nb:3 * nb]
        gate = jnp.where(past, gate, NEG_INF)
        rank = jnp.zeros((nb, blk), jnp.int32)
        for m in range(nb):
            gm = gate[m:m + 1, :]
            beats = (gm > gate) | ((gm == gate) & (m < nidx))
            rank = rank + beats.astype(jnp.int32)
        selected = past & (rank < MOBA_TOPK)
        rowbias_ref[h] = jnp.where(
            nidx == j, 0.0,
            jnp.where(selected, 0.0, NEG_INF) - slope * ((j - nidx) * blk + qcol).astype(F32))
        m_ref[h] = jnp.full((1, blk), NEG_INF, F32)
        l_ref[h] = jnp.zeros((1, blk), F32)
        acc_ref[h] = jnp.zeros((HEAD_DIM, blk), F32)

    def scores(n, h):
        s_ref[h] = jnp.dot(k_ref[0, n, :, lane_tile(h)], qm_ref[h], preferred_element_type=F32)

    def accumulate(n, own, h):
        t = s_ref[h] + bias_ref[own, h]
        rb = jnp.sum(jnp.where(nidx == n, rowbias_ref[h], 0.0), axis=0, keepdims=True)
        m = m_ref[h]
        m_new = jnp.maximum(m, _tree_max_rows(t) + rb)
        p = jnp.exp2(t - (m_new - rb))
        alpha = jnp.exp2(m - m_new)
        m_ref[h] = m_new
        lhs = jnp.concatenate([vt_ref[0, n, h * HEAD_DIM:(h + 1) * HEAD_DIM, :], ones], axis=0)
        pv = jnp.dot(lhs, p.astype(BF16), preferred_element_type=F32)
        l_ref[h] = alpha * l_ref[h] + pv[HEAD_DIM:HEAD_DIM + 1, :]
        acc_ref[h] = alpha * acc_ref[h] + pv[0:HEAD_DIM, :]

    for h in range(SCORE_LOOKAHEAD):
        scores(j, h)

    def body(i, carry):
        n = j - i
        n_next = jnp.maximum(n - 1, 0)
        own = (i == 0).astype(jnp.int32)
        for h in range(N_HEADS):
            if h + SCORE_LOOKAHEAD < N_HEADS:
                scores(n, h + SCORE_LOOKAHEAD)
            else:
                scores(n_next, h + SCORE_LOOKAHEAD - N_HEADS)
            accumulate(n, own, h)
        return carry

    lax.fori_loop(0, j + 1, body, 0)
    for t in range(N_HEADS // HEADS_PER_LANE_TILE):
        o_t = jnp.concatenate([acc_ref[h] / l_ref[h] for h in range(t * HEADS_PER_LANE_TILE,
                                                                     (t + 1) * HEADS_PER_LANE_TILE)], axis=0)
        o_ref[0, :, t * LANES:(t + 1) * LANES] = o_t.T.astype(o_ref.dtype)


def _moba_attention(slopes, qt, k4, vt, kmean):
    bsz, nb, _, blk = qt.shape
    return pl.pallas_call(
        _moba_kernel,
        grid=(bsz, nb),
        in_specs=[
            pl.BlockSpec(memory_space=pltpu.SMEM),
            pl.BlockSpec((1, 1, D_ATTN, blk), lambda b, j: (b, j, 0, 0)),
            pl.BlockSpec((1, nb, blk, D_ATTN), lambda b, j: (b, 0, 0, 0)),
            pl.BlockSpec((1, nb, D_ATTN, blk), lambda b, j: (b, 0, 0, 0)),
            pl.BlockSpec((1, nb, D_ATTN), lambda b, j: (b, 0, 0)),
        ],
        out_specs=pl.BlockSpec((1, blk, D_ATTN), lambda b, j: (b, j, 0)),
        out_shape=jax.ShapeDtypeStruct((bsz, nb * blk, D_ATTN), BF16),
        scratch_shapes=[
            pltpu.VMEM((N_HEADS, LANES, blk), BF16),
            pltpu.VMEM((2, N_HEADS, blk, blk), F32),
            pltpu.VMEM((N_HEADS, nb, blk), F32),
            pltpu.VMEM((N_HEADS, blk, blk), F32),
            pltpu.VMEM((N_HEADS, 1, blk), F32),
            pltpu.VMEM((N_HEADS, 1, blk), F32),
            pltpu.VMEM((N_HEADS, HEAD_DIM, blk), F32),
        ],
        compiler_params=pltpu.CompilerParams(
            dimension_semantics=("arbitrary", "arbitrary"), vmem_limit_bytes=ATTN_VMEM_LIMIT_BYTES),
        name="moba_attn",
    )(slopes, qt, k4, vt, kmean)


def _s5_kernel(u_ref, wb_ref, are_ref, aim_ref, wc_ref, d_ref, y_ref, xbuf, hstate):
    rows = u_ref.shape[0]
    steps = rows // SUBLANES
    n_tiles = wb_ref.shape[0]
    n_pairs = are_ref.shape[0]

    @pl.when(pl.program_id(0) == 0)
    def _():
        hstate[...] = jnp.zeros_like(hstate)

    for lt in range(n_tiles):
        u_t = u_ref[:, lt * LANES:(lt + 1) * LANES].astype(BF16)
        xbuf[:, lt * STATE_COLS_PER_LANE_TILE:(lt + 1) * STATE_COLS_PER_LANE_TILE] = jnp.dot(
            u_t, wb_ref[lt], preferred_element_type=F32)

    for p0 in range(0, n_pairs, SCAN_PAIRS_PER_LOOP):
        pairs = range(p0, p0 + SCAN_PAIRS_PER_LOOP)
        a_re = [jnp.broadcast_to(are_ref[pi:pi + 1, :], (SUBLANES, LANES)) for pi in pairs]
        a_im = [jnp.broadcast_to(aim_ref[pi:pi + 1, :], (SUBLANES, LANES)) for pi in pairs]
        cols = [pi * 2 * LANES for pi in pairs]
        init = tuple((hstate[:, c:c + LANES], hstate[:, c + LANES:c + 2 * LANES]) for c in cols)

        def step(t, hs):
            r = pl.multiple_of(t * SUBLANES, SUBLANES)
            out = []
            for c, ar, ai, (hr, hi) in zip(cols, a_re, a_im, hs):
                nr = ar * hr - ai * hi + xbuf[pl.ds(r, SUBLANES), c:c + LANES]
                ni = ar * hi + ai * hr + xbuf[pl.ds(r, SUBLANES), c + LANES:c + 2 * LANES]
                xbuf[pl.ds(r, SUBLANES), c:c + LANES] = nr
                xbuf[pl.ds(r, SUBLANES), c + LANES:c + 2 * LANES] = ni
                out.append((nr, ni))
            return tuple(out)

        final = lax.fori_loop(0, steps, step, init, unroll=2)
        for c, (hr, hi) in zip(cols, final):
            hstate[:, c:c + LANES] = hr
            hstate[:, c + LANES:c + 2 * LANES] = hi

    for lt in range(n_tiles):
        h_t = xbuf[:, lt * STATE_COLS_PER_LANE_TILE:(lt + 1) * STATE_COLS_PER_LANE_TILE].astype(BF16)
        lanes = slice(lt * LANES, (lt + 1) * LANES)
        y_ref[:, lanes] = (jnp.dot(h_t, wc_ref[lt], preferred_element_type=F32)
                           + d_ref[:, lanes] * u_ref[:, lanes])


def _s5_weights(lam_re, lam_im, log_dt, b_re, b_im, c_re, c_im):
    g, p = lam_re.shape
    n_tiles = g // GROUPS_PER_LANE_TILE
    dt = jnp.exp(log_dt)[:, None]
    mag = jnp.exp(lam_re * dt)
    ang = lam_im * dt
    a_re = mag * jnp.cos(ang)
    a_im = mag * jnp.sin(ang)
    den = lam_re * lam_re + lam_im * lam_im
    f_re = ((a_re - 1.0) * lam_re + a_im * lam_im) / den
    f_im = (a_im * lam_re - (a_re - 1.0) * lam_im) / den
    bb_re = f_re[..., None] * b_re - f_im[..., None] * b_im
    bb_im = f_re[..., None] * b_im + f_im[..., None] * b_re
    gl = np.arange(GROUPS_PER_LANE_TILE)[:, None, None]
    delta = jnp.asarray(gl == 2 * np.arange(PAIRS_PER_LANE_TILE)[None, :, None] + np.arange(2)[None, None, :], F32)
    bbs = jnp.stack([bb_re, bb_im]).reshape(2, n_tiles, GROUPS_PER_LANE_TILE, p, SSM_GROUP)
    wb = jnp.einsum('rlgpi,gkj->lgikrjp', bbs, delta).reshape(n_tiles, LANES, STATE_COLS_PER_LANE_TILE)
    cs = jnp.stack([c_re, -c_im]).reshape(2, n_tiles, GROUPS_PER_LANE_TILE, SSM_GROUP, p)
    wc = jnp.einsum('rlgop,gkj->lkrjpgo', cs, delta).reshape(n_tiles, STATE_COLS_PER_LANE_TILE, LANES)
    n_pairs = g // 2
    return (wb.astype(BF16), a_re.reshape(n_pairs, 2 * p), a_im.reshape(n_pairs, 2 * p), wc.astype(BF16))


def _s5_scan(us_tm, wb, a_re, a_im, wc, d_skip):
    rows_total, d_ssm = us_tm.shape
    rows = SCAN_STEPS * SUBLANES
    n_tiles = wb.shape[0]
    state_cols = n_tiles * STATE_COLS_PER_LANE_TILE
    const = lambda shape: pl.BlockSpec(shape, lambda c: (0,) * len(shape))
    return pl.pallas_call(
        _s5_kernel,
        grid=(rows_total // rows,),
        in_specs=[
            pl.BlockSpec((rows, d_ssm), lambda c: (c, 0)),
            const(wb.shape), const(a_re.shape), const(a_im.shape), const(wc.shape), const((1, d_ssm)),
        ],
        out_specs=pl.BlockSpec((rows, d_ssm), lambda c: (c, 0)),
        out_shape=jax.ShapeDtypeStruct((rows_total, d_ssm), F32),
        scratch_shapes=[pltpu.VMEM((rows, state_cols), F32), pltpu.VMEM((SUBLANES, state_cols), F32)],
        compiler_params=pltpu.CompilerParams(
            dimension_semantics=("arbitrary",), vmem_limit_bytes=VMEM_LIMIT_BYTES),
        name="s5_scan",
    )(us_tm, wb, a_re, a_im, wc, d_skip)


def _mix_ffn2_kernel(h_ref, attn_ref, y_ref, wglu, bglu, attn_g, ssm_g, wout, mixpost_g,
                     pre_g, wg, wu, wd, post_g, o_ref, act_ref):
    y = jax.nn.gelu(y_ref[...], approximate=True)
    z = y * jax.nn.sigmoid(jnp.dot(y.astype(BF16), wglu[...], preferred_element_type=F32) + bglu[...])
    zs = _rms(z, ssm_g[...]).astype(BF16)
    at = _rms(attn_ref[...].astype(F32), attn_g[...]).astype(BF16)
    mixed = (jnp.dot(at, wout[0:D_ATTN, :], preferred_element_type=F32)
             + jnp.dot(zs, wout[D_ATTN:, :], preferred_element_type=F32))
    h = h_ref[...] + _rms(mixed, mixpost_g[...])
    f = _swiglu(_rms(h, pre_g[...]).astype(BF16), wg, wu, wd, act_ref)
    o_ref[...] = h + 0.5 * _rms(f, post_g[...])


def _mix_ffn2(h2, attn2, y_tm, wglu, bglu, attn_g, ssm_g, wout, mixpost_g, pre_g, wg, wu, wd, post_g, seq):
    n, d = h2.shape
    d_ff = wg.shape[1]
    d_ssm = wglu.shape[0]
    tm = ROW_TILE
    tiles_per_seq = seq // tm
    const = lambda shape: pl.BlockSpec(shape, lambda i: (0,) * len(shape), pipeline_mode=pl.Buffered(1))
    return pl.pallas_call(
        _mix_ffn2_kernel,
        grid=(n // tm,),
        in_specs=[
            pl.BlockSpec((tm, d), lambda i: (i, 0)),
            pl.BlockSpec((tm, D_ATTN), lambda i: (i, 0)),
            pl.BlockSpec((tm, d_ssm), lambda i: (i % tiles_per_seq, i // tiles_per_seq)),
            const((d_ssm, d_ssm)), const((1, d_ssm)), const((1, D_ATTN)), const((1, d_ssm)),
            const((D_ATTN + d_ssm, d)), const((1, d)),
            const((1, d)), const((d, d_ff)), const((d, d_ff)), const((d_ff, d)), const((1, d)),
        ],
        out_specs=pl.BlockSpec((tm, d), lambda i: (i, 0)),
        out_shape=jax.ShapeDtypeStruct((n, d), F32),
        scratch_shapes=[pltpu.VMEM((tm, d_ff), BF16)],
        compiler_params=pltpu.CompilerParams(
            dimension_semantics=("arbitrary",), vmem_limit_bytes=VMEM_LIMIT_BYTES),
        name="mix_ffn2",
    )(h2, attn2, y_tm, wglu, bglu, attn_g, ssm_g, wout, mixpost_g, pre_g, wg, wu, wd, post_g)


def kernel(x, ffn1_pre_g, ffn1_w_gate, ffn1_w_up, ffn1_w_down, ffn1_post_g, mix_pre_g, w_in, lam_re, lam_im, log_dt, b_re, b_im, c_re, c_im, d_skip, w_glu, b_glu, attn_out_g, ssm_out_g, w_out, mix_post_g, ffn2_pre_g, ffn2_w_gate, ffn2_w_up, ffn2_w_down, ffn2_post_g):
    bsz, seq, d = x.shape
    depth = w_in.shape[0]
    d_ssm = w_glu.shape[1]
    assert w_in.shape[2] == 3 * D_ATTN + d_ssm and d_ssm == lam_re.shape[1] * SSM_GROUP
    assert seq % ROW_TILE == 0 and ROW_TILE % MOBA_BLOCK == 0 and seq % SCAN_STEPS == 0
    assert bsz == SUBLANES and lam_re.shape[2] == SSM_STATE
    nb = seq // MOBA_BLOCK
    slopes = jnp.asarray(LOG2E * 2.0 ** (-8.0 * np.arange(1, N_HEADS + 1) / N_HEADS), dtype=F32)
    row = lambda g: g.reshape(1, -1).astype(F32)
    bf = lambda w: w.astype(BF16)

    h2 = x.reshape(bsz * seq, d)
    for l in range(depth):
        h2, qt, k2, vt, us_tm, kmean = _ffn1_proj(
            h2, row(ffn1_pre_g[l]), bf(ffn1_w_gate[l]), bf(ffn1_w_up[l]), bf(ffn1_w_down[l]),
            row(ffn1_post_g[l]), row(mix_pre_g[l]), bf(w_in[l]), bsz, seq)
        attn = _moba_attention(slopes, qt, k2.reshape(bsz, nb, MOBA_BLOCK, D_ATTN), vt,
                               kmean.reshape(bsz, nb, D_ATTN))
        wb, a_re, a_im, wc = _s5_weights(lam_re[l], lam_im[l], log_dt[l], b_re[l], b_im[l], c_re[l], c_im[l])
        y_tm = _s5_scan(us_tm.reshape(seq * bsz, d_ssm), wb, a_re, a_im, wc, row(d_skip[l]))
        h2 = _mix_ffn2(
            h2, attn.reshape(bsz * seq, D_ATTN), y_tm.reshape(seq, bsz * d_ssm),
            bf(w_glu[l]), row(b_glu[l]), row(attn_out_g[l]), row(ssm_out_g[l]), bf(w_out[l]),
            row(mix_post_g[l]), row(ffn2_pre_g[l]), bf(ffn2_w_gate[l]), bf(ffn2_w_up[l]),
            bf(ffn2_w_down[l]), row(ffn2_post_g[l]), seq)
    return h2.reshape(bsz, seq, d)
```

```python
import functools
import math

import numpy as np
import jax
import jax.numpy as jnp
from jax import lax
from jax.experimental import pallas as pl
from jax.experimental.pallas import tpu as pltpu

F32 = jnp.float32
BF16 = jnp.bfloat16

N_HEADS = 8
HEAD_DIM = 64
D_ATTN = N_HEADS * HEAD_DIM
MOBA_BLOCK = 256
MOBA_TOPK = 3
SSM_GROUP = 16
SSM_STATE = 64
RMS_EPS = 1e-6
NEG_INF = -1e30

LANES = 128
SUBLANES = 8
BF16_SUBLANES = 16
HEADS_PER_LANE_TILE = LANES // HEAD_DIM
GROUPS_PER_LANE_TILE = LANES // SSM_GROUP
PAIRS_PER_LANE_TILE = GROUPS_PER_LANE_TILE // 2
STATE_COLS_PER_LANE_TILE = GROUPS_PER_LANE_TILE * 2 * SSM_STATE

ROW_TILE = 512
SCAN_STEPS = 64
SCAN_PAIRS_PER_LOOP = 4
SCORE_LOOKAHEAD = 4
VMEM_LIMIT_BYTES = 56 * 1024 * 1024
ATTN_VMEM_LIMIT_BYTES = 32 * 1024 * 1024
LOG2E = math.log2(math.e)


def _rms(x, g):
    return x * lax.rsqrt(jnp.mean(x * x, axis=-1, keepdims=True) + RMS_EPS) * g


def _ff_chunks(d_ff, width=768):
    return [(s, min(width, d_ff - s)) for s in range(0, d_ff, width)]


def _swiglu(xn, wg_ref, wu_ref, wd_ref, act_ref):
    for s, w in _ff_chunks(wg_ref.shape[1]):
        g = jnp.dot(xn, wg_ref[:, s:s + w], preferred_element_type=F32)
        u = jnp.dot(xn, wu_ref[:, s:s + w], preferred_element_type=F32)
        act_ref[:, s:s + w] = (g * jax.nn.sigmoid(g) * u).astype(BF16)
    return jnp.dot(act_ref[...], wd_ref[...], preferred_element_type=F32)


def _ffn1_proj_kernel(x_ref, pre_g, wg, wu, wd, post_g, mix_g, win,
                      h_ref, qt_ref, k_ref, vt_ref, us_ref, kmean_ref, act_ref):
    x = x_ref[...]
    f = _swiglu(_rms(x, pre_g[...]).astype(BF16), wg, wu, wd, act_ref)
    h = x + 0.5 * _rms(f, post_g[...])
    h_ref[...] = h
    u = _rms(h, mix_g[...]).astype(BF16)
    q = jnp.dot(u, win[:, 0:D_ATTN], preferred_element_type=F32) * (LOG2E / math.sqrt(HEAD_DIM))
    k = jnp.dot(u, win[:, D_ATTN:2 * D_ATTN], preferred_element_type=F32)
    v = jnp.dot(u, win[:, 2 * D_ATTN:3 * D_ATTN], preferred_element_type=F32)
    us = jnp.dot(u, win[:, 3 * D_ATTN:], preferred_element_type=F32)
    for lt in range(us_ref.shape[0]):
        us_ref[lt] = us[:, lt * LANES:(lt + 1) * LANES]
    k_ref[...] = k.astype(BF16)
    for blk in range(x.shape[0] // MOBA_BLOCK):
        rows = slice(blk * MOBA_BLOCK, (blk + 1) * MOBA_BLOCK)
        qt_ref[0, blk] = q[rows].T.astype(BF16)
        vt_ref[0, blk] = v[rows].T.astype(BF16)
        kmean_ref[0, blk:blk + 1, :] = jnp.mean(k[rows], axis=0, keepdims=True)


def _ffn1_proj(x2, pre_g, wg, wu, wd, post_g, mix_g, win, bsz, seq):
    n, d = x2.shape
    d_ff = wg.shape[1]
    d_in = win.shape[1]
    d_ssm = d_in - 3 * D_ATTN
    tm = ROW_TILE
    tiles_per_seq = seq // tm
    blocks_per_tile = tm // MOBA_BLOCK
    nb = seq // MOBA_BLOCK
    const = lambda shape: pl.BlockSpec(shape, lambda i: (0,) * len(shape), pipeline_mode=pl.Buffered(1))
    return pl.pallas_call(
        _ffn1_proj_kernel,
        grid=(n // tm,),
        in_specs=[
            pl.BlockSpec((tm, d), lambda i: (i, 0)),
            const((1, d)), const((d, d_ff)), const((d, d_ff)), const((d_ff, d)), const((1, d)),
            const((1, d)), const((d, d_in)),
        ],
        out_specs=[
            pl.BlockSpec((tm, d), lambda i: (i, 0)),
            pl.BlockSpec((1, blocks_per_tile, D_ATTN, MOBA_BLOCK),
                         lambda i: (i // tiles_per_seq, i % tiles_per_seq, 0, 0)),
            pl.BlockSpec((tm, D_ATTN), lambda i: (i, 0)),
            pl.BlockSpec((1, blocks_per_tile, D_ATTN, MOBA_BLOCK),
                         lambda i: (i // tiles_per_seq, i % tiles_per_seq, 0, 0)),
            pl.BlockSpec((d_ssm // LANES, tm, LANES), lambda i: (0, i, 0)),
            pl.BlockSpec((1, blocks_per_tile, D_ATTN), lambda i: (i, 0, 0)),
        ],
        out_shape=[
            jax.ShapeDtypeStruct((n, d), F32),
            jax.ShapeDtypeStruct((bsz, nb, D_ATTN, MOBA_BLOCK), BF16),
            jax.ShapeDtypeStruct((n, D_ATTN), BF16),
            jax.ShapeDtypeStruct((bsz, nb, D_ATTN, MOBA_BLOCK), BF16),
            jax.ShapeDtypeStruct((d_ssm // LANES, n, LANES), F32),
            jax.ShapeDtypeStruct((n // tm, blocks_per_tile, D_ATTN), F32),
        ],
        scratch_shapes=[pltpu.VMEM((tm, d_ff), BF16)],
        compiler_params=pltpu.CompilerParams(
            dimension_semantics=("arbitrary",), vmem_limit_bytes=VMEM_LIMIT_BYTES),
        name="ffn1_proj",
    )(x2, pre_g, wg, wu, wd, post_g, mix_g, win)


def _tree_max_rows(t):
    while t.shape[0] > SUBLANES:
        half = t.shape[0] // 2
        t = jnp.maximum(t[:half], t[half:])
    return jnp.max(t, axis=0, keepdims=True)


def _moba_kernel(slopes_ref, qt_ref, k_ref, vt_ref, kmean_ref, o_ref,
                 qm_ref, bias_ref, rowbias_ref, s_ref, m_ref, l_ref, acc_ref):
    j = pl.program_id(1)
    blk = MOBA_BLOCK
    nb = k_ref.shape[1]
    kpos = lax.broadcasted_iota(jnp.int32, (blk, blk), 0)
    qpos = lax.broadcasted_iota(jnp.int32, (blk, blk), 1)
    nidx = lax.broadcasted_iota(jnp.int32, (nb, blk), 0)
    qcol = lax.broadcasted_iota(jnp.int32, (nb, blk), 1)
    chan = lax.broadcasted_iota(jnp.int32, (LANES, blk), 0)
    past = nidx < j
    ones = jnp.ones((BF16_SUBLANES, blk), BF16)

    @pl.when((pl.program_id(0) == 0) & (j == 0))
    def _():
        for h in range(N_HEADS):
            slope = slopes_ref[h]
            bias_ref[0, h] = slope * kpos.astype(F32)
            bias_ref[1, h] = jnp.where(qpos >= kpos, slope * (kpos - qpos).astype(F32), NEG_INF)

    def lane_tile(h):
        t = h // HEADS_PER_LANE_TILE
        return slice(t * LANES, (t + 1) * LANES)

    for h in range(N_HEADS):
        slope = slopes_ref[h]
        lanes = lane_tile(h)
        lo = (h % HEADS_PER_LANE_TILE) * HEAD_DIM
        qt2 = qt_ref[0, 0, lanes, :]
        qm = jnp.where((chan >= lo) & (chan < lo + HEAD_DIM), qt2, jnp.zeros_like(qt2))
        qm_ref[h] = qm
        km = kmean_ref[0, :, lanes]
        km_hi = km.astype(BF16).astype(F32)
        km_mid = (km - km_hi).astype(BF16).astype(F32)
        km_lo = km - km_hi - km_mid
        gate3 = jnp.dot(jnp.concatenate([km_hi, km_mid, km_lo], axis=0).astype(BF16), qm,
                        preferred_element_type=F32)
        gate = gate3[0:nb] + gate3[nb:2 * nb] + gate3[2 * nb:3 * nb]
        gate = jnp.where(past, gate, NEG_INF)
        rank = jnp.zeros((nb, blk), jnp.int32)
        for m in range(nb):
            gm = gate[m:m + 1, :]
            beats = (gm > gate) | ((gm == gate) & (m < nidx))
            rank = rank + beats.astype(jnp.int32)
        selected = past & (rank < MOBA_TOPK)
        rowbias_ref[h] = jnp.where(
            nidx == j, 0.0,
            jnp.where(selected, 0.0, NEG_INF) - slope * ((j - nidx) * blk + qcol).astype(F32))
        m_ref[h] = jnp.full((1, blk), NEG_INF, F32)
        l_ref[h] = jnp.zeros((1, blk), F32)
        acc_ref[h] = jnp.zeros((HEAD_DIM, blk), F32)

    def scores(n, h):
        s_ref[h] = jnp.dot(k_ref[0, n, :, lane_tile(h)], qm_ref[h], preferred_element_type=F32)

    def accumulate(n, own, h):
        t = s_ref[h] + bias_ref[own, h]
        rb = jnp.sum(jnp.where(nidx == n, rowbias_ref[h], 0.0), axis=0, keepdims=True)
        m = m_ref[h]
        m_new = jnp.maximum(m, _tree_max_rows(t) + rb)
        p = jnp.exp2(t - (m_new - rb))
        alpha = jnp.exp2(m - m_new)
        m_ref[h] = m_new
        lhs = jnp.concatenate([vt_ref[0, n, h * HEAD_DIM:(h + 1) * HEAD_DIM, :], ones], axis=0)
        pv = jnp.dot(lhs, p.astype(BF16), preferred_element_type=F32)
        l_ref[h] = alpha * l_ref[h] + pv[HEAD_DIM:HEAD_DIM + 1, :]
        acc_ref[h] = alpha * acc_ref[h] + pv[0:HEAD_DIM, :]

    for h in range(SCORE_LOOKAHEAD):
        scores(j, h)

    def body(i, carry):
        n = j - i
        n_next = jnp.maximum(n - 1, 0)
        own = (i == 0).astype(jnp.int32)
        for h in range(N_HEADS):
            if h + SCORE_LOOKAHEAD < N_HEADS:
                scores(n, h + SCORE_LOOKAHEAD)
            else:
                scores(n_next, h + SCORE_LOOKAHEAD - N_HEADS)
            accumulate(n, own, h)
        return carry

    lax.fori_loop(0, j + 1, body, 0)
    for t in range(N_HEADS // HEADS_PER_LANE_TILE):
        o_t = jnp.concatenate([acc_ref[h] / l_ref[h] for h in range(t * HEADS_PER_LANE_TILE,
                                                                     (t + 1) * HEADS_PER_LANE_TILE)], axis=0)
        o_ref[0, :, t * LANES:(t + 1) * LANES] = o_t.T.astype(o_ref.dtype)


def _moba_attention(slopes, qt, k4, vt, kmean):
    bsz, nb, _, blk = qt.shape
    return pl.pallas_call(
        _moba_kernel,
        grid=(bsz, nb),
        in_specs=[
            pl.BlockSpec(memory_space=pltpu.SMEM),
            pl.BlockSpec((1, 1, D_ATTN, blk), lambda b, j: (b, j, 0, 0)),
            pl.BlockSpec((1, nb, blk, D_ATTN), lambda b, j: (b, 0, 0, 0)),
            pl.BlockSpec((1, nb, D_ATTN, blk), lambda b, j: (b, 0, 0, 0)),
            pl.BlockSpec((1, nb, D_ATTN), lambda b, j: (b, 0, 0)),
        ],
        out_specs=pl.BlockSpec((1, blk, D_ATTN), lambda b, j: (b, j, 0)),
        out_shape=jax.ShapeDtypeStruct((bsz, nb * blk, D_ATTN), BF16),
        scratch_shapes=[
            pltpu.VMEM((N_HEADS, LANES, blk), BF16),
            pltpu.VMEM((2, N_HEADS, blk, blk), F32),
            pltpu.VMEM((N_HEADS, nb, blk), F32),
            pltpu.VMEM((N_HEADS, blk, blk), F32),
            pltpu.VMEM((N_HEADS, 1, blk), F32),
            pltpu.VMEM((N_HEADS, 1, blk), F32),
            pltpu.VMEM((N_HEADS, HEAD_DIM, blk), F32),
        ],
        compiler_params=pltpu.CompilerParams(
            dimension_semantics=("arbitrary", "arbitrary"), vmem_limit_bytes=ATTN_VMEM_LIMIT_BYTES),
        name="moba_attn",
    )(slopes, qt, k4, vt, kmean)


def _s5_kernel(u_ref, wb_ref, are_ref, aim_ref, wc_ref, d_ref, y_ref, ubuf, ybuf, xbuf, hstate):
    n_tiles, bsz, steps, _ = u_ref.shape
    rows = bsz * steps
    n_pairs = are_ref.shape[0]
    u2 = u_ref.reshape(n_tiles, rows, LANES)
    y2 = y_ref.reshape(n_tiles, rows, LANES)

    @pl.when(pl.program_id(0) == 0)
    def _():
        hstate[...] = jnp.zeros_like(hstate)

    for lt in range(n_tiles):
        for t in range(steps):
            ubuf[t * bsz:(t + 1) * bsz, lt * LANES:(lt + 1) * LANES] = u2[lt, pl.ds(t, bsz, stride=steps), :]

    for lt in range(n_tiles):
        u_t = ubuf[:, lt * LANES:(lt + 1) * LANES].astype(BF16)
        xbuf[:, lt * STATE_COLS_PER_LANE_TILE:(lt + 1) * STATE_COLS_PER_LANE_TILE] = jnp.dot(
            u_t, wb_ref[lt], preferred_element_type=F32)

    for p0 in range(0, n_pairs, SCAN_PAIRS_PER_LOOP):
        pairs = range(p0, p0 + SCAN_PAIRS_PER_LOOP)
        a_re = [jnp.broadcast_to(are_ref[pi:pi + 1, :], (SUBLANES, LANES)) for pi in pairs]
        a_im = [jnp.broadcast_to(aim_ref[pi:pi + 1, :], (SUBLANES, LANES)) for pi in pairs]
        cols = [pi * 2 * LANES for pi in pairs]
        init = tuple((hstate[:, c:c + LANES], hstate[:, c + LANES:c + 2 * LANES]) for c in cols)

        def step(t, hs):
            r = pl.multiple_of(t * SUBLANES, SUBLANES)
            out = []
            for c, ar, ai, (hr, hi) in zip(cols, a_re, a_im, hs):
                nr = ar * hr - ai * hi + xbuf[pl.ds(r, SUBLANES), c:c + LANES]
                ni = ar * hi + ai * hr + xbuf[pl.ds(r, SUBLANES), c + LANES:c + 2 * LANES]
                xbuf[pl.ds(r, SUBLANES), c:c + LANES] = nr
                xbuf[pl.ds(r, SUBLANES), c + LANES:c + 2 * LANES] = ni
                out.append((nr, ni))
            return tuple(out)

        final = lax.fori_loop(0, steps, step, init, unroll=2)
        for c, (hr, hi) in zip(cols, final):
            hstate[:, c:c + LANES] = hr
            hstate[:, c + LANES:c + 2 * LANES] = hi

    for lt in range(n_tiles):
        h_t = xbuf[:, lt * STATE_COLS_PER_LANE_TILE:(lt + 1) * STATE_COLS_PER_LANE_TILE].astype(BF16)
        lanes = slice(lt * LANES, (lt + 1) * LANES)
        ybuf[:, lanes] = (jnp.dot(h_t, wc_ref[lt], preferred_element_type=F32)
                          + d_ref[:, lanes] * ubuf[:, lanes])
    for lt in range(n_tiles):
        for t in range(steps):
            y2[lt, pl.ds(t, bsz, stride=steps), :] = ybuf[t * bsz:(t + 1) * bsz, lt * LANES:(lt + 1) * LANES]


def _s5_weights(lam_re, lam_im, log_dt, b_re, b_im, c_re, c_im):
    g, p = lam_re.shape
    n_tiles = g // GROUPS_PER_LANE_TILE
    dt = jnp.exp(log_dt)[:, None]
    mag = jnp.exp(lam_re * dt)
    ang = lam_im * dt
    a_re = mag * jnp.cos(ang)
    a_im = mag * jnp.sin(ang)
    den = lam_re * lam_re + lam_im * lam_im
    f_re = ((a_re - 1.0) * lam_re + a_im * lam_im) / den
    f_im = (a_im * lam_re - (a_re - 1.0) * lam_im) / den
    bb_re = f_re[..., None] * b_re - f_im[..., None] * b_im
    bb_im = f_re[..., None] * b_im + f_im[..., None] * b_re
    gl = np.arange(GROUPS_PER_LANE_TILE)[:, None, None]
    delta = jnp.asarray(gl == 2 * np.arange(PAIRS_PER_LANE_TILE)[None, :, None] + np.arange(2)[None, None, :], F32)
    bbs = jnp.stack([bb_re, bb_im]).reshape(2, n_tiles, GROUPS_PER_LANE_TILE, p, SSM_GROUP)
    wb = jnp.einsum('rlgpi,gkj->lgikrjp', bbs, delta).reshape(n_tiles, LANES, STATE_COLS_PER_LANE_TILE)
    cs = jnp.stack([c_re, -c_im]).reshape(2, n_tiles, GROUPS_PER_LANE_TILE, SSM_GROUP, p)
    wc = jnp.einsum('rlgop,gkj->lkrjpgo', cs, delta).reshape(n_tiles, STATE_COLS_PER_LANE_TILE, LANES)
    n_pairs = g // 2
    return (wb.astype(BF16), a_re.reshape(n_pairs, 2 * p), a_im.reshape(n_pairs, 2 * p), wc.astype(BF16))


def _s5_scan(us, wb, a_re, a_im, wc, d_skip):
    n_tiles, bsz, seq, _ = us.shape
    d_ssm = n_tiles * LANES
    steps = SCAN_STEPS
    rows = steps * bsz
    state_cols = n_tiles * STATE_COLS_PER_LANE_TILE
    const = lambda shape: pl.BlockSpec(shape, lambda c: (0,) * len(shape))
    return pl.pallas_call(
        _s5_kernel,
        grid=(seq // steps,),
        in_specs=[
            pl.BlockSpec((n_tiles, bsz, steps, LANES), lambda c: (0, 0, c, 0)),
            const(wb.shape), const(a_re.shape), const(a_im.shape), const(wc.shape), const((1, d_ssm)),
        ],
        out_specs=pl.BlockSpec((n_tiles, bsz, steps, LANES), lambda c: (0, 0, c, 0)),
        out_shape=jax.ShapeDtypeStruct((n_tiles, bsz, seq, LANES), F32),
        scratch_shapes=[pltpu.VMEM((rows, d_ssm), F32), pltpu.VMEM((rows, d_ssm), F32),
                        pltpu.VMEM((rows, state_cols), F32), pltpu.VMEM((bsz, state_cols), F32)],
        compiler_params=pltpu.CompilerParams(
            dimension_semantics=("arbitrary",), vmem_limit_bytes=VMEM_LIMIT_BYTES),
        name="s5_scan",
    )(us, wb, a_re, a_im, wc, d_skip)


def _mix_ffn2_kernel(h_ref, attn_ref, y_ref, wglu, bglu, attn_g, ssm_g, wout, mixpost_g,
                     pre_g, wg, wu, wd, post_g, o_ref, act_ref):
    y = jnp.concatenate([y_ref[lt] for lt in range(y_ref.shape[0])], axis=-1)
    y = jax.nn.gelu(y, approximate=True)
    z = y * jax.nn.sigmoid(jnp.dot(y.astype(BF16), wglu[...], preferred_element_type=F32) + bglu[...])
    zs = _rms(z, ssm_g[...]).astype(BF16)
    at = _rms(attn_ref[...].astype(F32), attn_g[...]).astype(BF16)
    mixed = (jnp.dot(at, wout[0:D_ATTN, :], preferred_element_type=F32)
             + jnp.dot(zs, wout[D_ATTN:, :], preferred_element_type=F32))
    h = h_ref[...] + _rms(mixed, mixpost_g[...])
    f = _swiglu(_rms(h, pre_g[...]).astype(BF16), wg, wu, wd, act_ref)
    o_ref[...] = h + 0.5 * _rms(f, post_g[...])


def _mix_ffn2(h2, attn2, y2, wglu, bglu, attn_g, ssm_g, wout, mixpost_g, pre_g, wg, wu, wd, post_g, seq):
    n, d = h2.shape
    d_ff = wg.shape[1]
    d_ssm = wglu.shape[0]
    tm = ROW_TILE
    tiles_per_seq = seq // tm
    const = lambda shape: pl.BlockSpec(shape, lambda i: (0,) * len(shape), pipeline_mode=pl.Buffered(1))
    return pl.pallas_call(
        _mix_ffn2_kernel,
        grid=(n // tm,),
        in_specs=[
            pl.BlockSpec((tm, d), lambda i: (i, 0)),
            pl.BlockSpec((tm, D_ATTN), lambda i: (i, 0)),
            pl.BlockSpec((d_ssm // LANES, tm, LANES), lambda i: (0, i, 0)),
            const((d_ssm, d_ssm)), const((1, d_ssm)), const((1, D_ATTN)), const((1, d_ssm)),
            const((D_ATTN + d_ssm, d)), const((1, d)),
            const((1, d)), const((d, d_ff)), const((d, d_ff)), const((d_ff, d)), const((1, d)),
        ],
        out_specs=pl.BlockSpec((tm, d), lambda i: (i, 0)),
        out_shape=jax.ShapeDtypeStruct((n, d), F32),
        scratch_shapes=[pltpu.VMEM((tm, d_ff), BF16)],
        compiler_params=pltpu.CompilerParams(
            dimension_semantics=("arbitrary",), vmem_limit_bytes=VMEM_LIMIT_BYTES),
        name="mix_ffn2",
    )(h2, attn2, y2, wglu, bglu, attn_g, ssm_g, wout, mixpost_g, pre_g, wg, wu, wd, post_g)


def kernel(x, ffn1_pre_g, ffn1_w_gate, ffn1_w_up, ffn1_w_down, ffn1_post_g, mix_pre_g, w_in, lam_re, lam_im, log_dt, b_re, b_im, c_re, c_im, d_skip, w_glu, b_glu, attn_out_g, ssm_out_g, w_out, mix_post_g, ffn2_pre_g, ffn2_w_gate, ffn2_w_up, ffn2_w_down, ffn2_post_g):
    bsz, seq, d = x.shape
    depth = w_in.shape[0]
    d_ssm = w_glu.shape[1]
    assert w_in.shape[2] == 3 * D_ATTN + d_ssm and d_ssm == lam_re.shape[1] * SSM_GROUP
    assert seq % ROW_TILE == 0 and ROW_TILE % MOBA_BLOCK == 0 and seq % SCAN_STEPS == 0
    assert bsz == SUBLANES and lam_re.shape[2] == SSM_STATE
    nb = seq // MOBA_BLOCK
    slopes = jnp.asarray(LOG2E * 2.0 ** (-8.0 * np.arange(1, N_HEADS + 1) / N_HEADS), dtype=F32)
    row = lambda g: g.reshape(1, -1).astype(F32)
    bf = lambda w: w.astype(BF16)

    h2 = x.reshape(bsz * seq, d)
    for l in range(depth):
        h2, qt, k2, vt, us, kmean = _ffn1_proj(
            h2, row(ffn1_pre_g[l]), bf(ffn1_w_gate[l]), bf(ffn1_w_up[l]), bf(ffn1_w_down[l]),
            row(ffn1_post_g[l]), row(mix_pre_g[l]), bf(w_in[l]), bsz, seq)
        attn = _moba_attention(slopes, qt, k2.reshape(bsz, nb, MOBA_BLOCK, D_ATTN), vt,
                               kmean.reshape(bsz, nb, D_ATTN))
        wb, a_re, a_im, wc = _s5_weights(lam_re[l], lam_im[l], log_dt[l], b_re[l], b_im[l], c_re[l], c_im[l])
        y = _s5_scan(us.reshape(-1, bsz, seq, LANES), wb, a_re, a_im, wc, row(d_skip[l]))
        h2 = _mix_ffn2(
            h2, attn.reshape(bsz * seq, D_ATTN), y.reshape(-1, bsz * seq, LANES),
            bf(w_glu[l]), row(b_glu[l]), row(attn_out_g[l]), row(ssm_out_g[l]), bf(w_out[l]),
            row(mix_post_g[l]), row(ffn2_pre_g[l]), bf(ffn2_w_gate[l]), bf(ffn2_w_up[l]),
            bf(ffn2_w_down[l]), row(ffn2_post_g[l]), seq)
    return h2.reshape(bsz, seq, d)
```

```python
import functools
import math

import numpy as np
import jax
import jax.numpy as jnp
from jax import lax
from jax.experimental import pallas as pl
from jax.experimental.pallas import tpu as pltpu

F32 = jnp.float32
BF16 = jnp.bfloat16

N_HEADS = 8
HEAD_DIM = 64
D_ATTN = N_HEADS * HEAD_DIM
MOBA_BLOCK = 256
MOBA_TOPK = 3
SSM_GROUP = 16
SSM_STATE = 64
RMS_EPS = 1e-6
NEG_INF = -1e30

LANES = 128
SUBLANES = 8
BF16_SUBLANES = 16
HEADS_PER_LANE_TILE = LANES // HEAD_DIM
GROUPS_PER_LANE_TILE = LANES // SSM_GROUP
PAIRS_PER_LANE_TILE = GROUPS_PER_LANE_TILE // 2
STATE_COLS_PER_LANE_TILE = GROUPS_PER_LANE_TILE * 2 * SSM_STATE

ROW_TILE = 512
SCAN_STEPS = 64
SCORE_LOOKAHEAD = 4
VMEM_LIMIT_BYTES = 56 * 1024 * 1024
ATTN_VMEM_LIMIT_BYTES = 32 * 1024 * 1024
LOG2E = math.log2(math.e)


def _rms(x, g):
    return x * lax.rsqrt(jnp.mean(x * x, axis=-1, keepdims=True) + RMS_EPS) * g


def _ff_chunks(d_ff, width=768):
    return [(s, min(width, d_ff - s)) for s in range(0, d_ff, width)]


def _swiglu(xn, wg_ref, wu_ref, wd_ref, act_ref):
    for s, w in _ff_chunks(wg_ref.shape[1]):
        g = jnp.dot(xn, wg_ref[:, s:s + w], preferred_element_type=F32)
        u = jnp.dot(xn, wu_ref[:, s:s + w], preferred_element_type=F32)
        act_ref[:, s:s + w] = (g * jax.nn.sigmoid(g) * u).astype(BF16)
    return jnp.dot(act_ref[...], wd_ref[...], preferred_element_type=F32)


def _ffn1_proj_kernel(x_ref, pre_g, wg, wu, wd, post_g, mix_g, win,
                      h_ref, qt_ref, k_ref, vt_ref, us_ref, kmean_ref, act_ref):
    x = x_ref[...]
    f = _swiglu(_rms(x, pre_g[...]).astype(BF16), wg, wu, wd, act_ref)
    h = x + 0.5 * _rms(f, post_g[...])
    h_ref[...] = h
    u = _rms(h, mix_g[...]).astype(BF16)
    q = jnp.dot(u, win[:, 0:D_ATTN], preferred_element_type=F32) * (LOG2E / math.sqrt(HEAD_DIM))
    k = jnp.dot(u, win[:, D_ATTN:2 * D_ATTN], preferred_element_type=F32)
    v = jnp.dot(u, win[:, 2 * D_ATTN:3 * D_ATTN], preferred_element_type=F32)
    us = jnp.dot(u, win[:, 3 * D_ATTN:], preferred_element_type=F32)
    for lt in range(us_ref.shape[0]):
        us_ref[lt] = us[:, lt * LANES:(lt + 1) * LANES]
    k_ref[...] = k.astype(BF16)
    for blk in range(x.shape[0] // MOBA_BLOCK):
        rows = slice(blk * MOBA_BLOCK, (blk + 1) * MOBA_BLOCK)
        qt_ref[0, blk] = q[rows].T.astype(BF16)
        vt_ref[0, blk] = v[rows].T.astype(BF16)
        kmean_ref[0, blk:blk + 1, :] = jnp.mean(k[rows], axis=0, keepdims=True)


def _ffn1_proj(x2, pre_g, wg, wu, wd, post_g, mix_g, win, bsz, seq):
    n, d = x2.shape
    d_ff = wg.shape[1]
    d_in = win.shape[1]
    d_ssm = d_in - 3 * D_ATTN
    tm = ROW_TILE
    tiles_per_seq = seq // tm
    blocks_per_tile = tm // MOBA_BLOCK
    nb = seq // MOBA_BLOCK
    const = lambda shape: pl.BlockSpec(shape, lambda i: (0,) * len(shape), pipeline_mode=pl.Buffered(1))
    return pl.pallas_call(
        _ffn1_proj_kernel,
        grid=(n // tm,),
        in_specs=[
            pl.BlockSpec((tm, d), lambda i: (i, 0)),
            const((1, d)), const((d, d_ff)), const((d, d_ff)), const((d_ff, d)), const((1, d)),
            const((1, d)), const((d, d_in)),
        ],
        out_specs=[
            pl.BlockSpec((tm, d), lambda i: (i, 0)),
            pl.BlockSpec((1, blocks_per_tile, D_ATTN, MOBA_BLOCK),
                         lambda i: (i // tiles_per_seq, i % tiles_per_seq, 0, 0)),
            pl.BlockSpec((tm, D_ATTN), lambda i: (i, 0)),
            pl.BlockSpec((1, blocks_per_tile, D_ATTN, MOBA_BLOCK),
                         lambda i: (i // tiles_per_seq, i % tiles_per_seq, 0, 0)),
            pl.BlockSpec((d_ssm // LANES, tm, LANES), lambda i: (0, i, 0)),
            pl.BlockSpec((1, blocks_per_tile, D_ATTN), lambda i: (i, 0, 0)),
        ],
        out_shape=[
            jax.ShapeDtypeStruct((n, d), F32),
            jax.ShapeDtypeStruct((bsz, nb, D_ATTN, MOBA_BLOCK), BF16),
            jax.ShapeDtypeStruct((n, D_ATTN), BF16),
            jax.ShapeDtypeStruct((bsz, nb, D_ATTN, MOBA_BLOCK), BF16),
            jax.ShapeDtypeStruct((d_ssm // LANES, n, LANES), F32),
            jax.ShapeDtypeStruct((n // tm, blocks_per_tile, D_ATTN), F32),
        ],
        scratch_shapes=[pltpu.VMEM((tm, d_ff), BF16)],
        compiler_params=pltpu.CompilerParams(
            dimension_semantics=("arbitrary",), vmem_limit_bytes=VMEM_LIMIT_BYTES),
        name="ffn1_proj",
    )(x2, pre_g, wg, wu, wd, post_g, mix_g, win)


def _tree_max_rows(t):
    while t.shape[0] > SUBLANES:
        half = t.shape[0] // 2
        t = jnp.maximum(t[:half], t[half:])
    return jnp.max(t, axis=0, keepdims=True)


def _moba_kernel(slopes_ref, qt_ref, k_ref, vt_ref, kmean_ref, o_ref,
                 qm_ref, bias_ref, rowbias_ref, s_ref, m_ref, l_ref, acc_ref):
    j = pl.program_id(1)
    blk = MOBA_BLOCK
    nb = k_ref.shape[1]
    kpos = lax.broadcasted_iota(jnp.int32, (blk, blk), 0)
    qpos = lax.broadcasted_iota(jnp.int32, (blk, blk), 1)
    nidx = lax.broadcasted_iota(jnp.int32, (nb, blk), 0)
    qcol = lax.broadcasted_iota(jnp.int32, (nb, blk), 1)
    chan = lax.broadcasted_iota(jnp.int32, (LANES, blk), 0)
    past = nidx < j
    ones = jnp.ones((BF16_SUBLANES, blk), BF16)

    @pl.when((pl.program_id(0) == 0) & (j == 0))
    def _():
        for h in range(N_HEADS):
            slope = slopes_ref[h]
            bias_ref[0, h] = slope * kpos.astype(F32)
            bias_ref[1, h] = jnp.where(qpos >= kpos, slope * (kpos - qpos).astype(F32), NEG_INF)

    def lane_tile(h):
        t = h // HEADS_PER_LANE_TILE
        return slice(t * LANES, (t + 1) * LANES)

    for h in range(N_HEADS):
        slope = slopes_ref[h]
        lanes = lane_tile(h)
        lo = (h % HEADS_PER_LANE_TILE) * HEAD_DIM
        qt2 = qt_ref[0, 0, lanes, :]
        qm = jnp.where((chan >= lo) & (chan < lo + HEAD_DIM), qt2, jnp.zeros_like(qt2))
        qm_ref[h] = qm
        km = kmean_ref[0, :, lanes]
        km_hi = km.astype(BF16).astype(F32)
        km_mid = (km - km_hi).astype(BF16).astype(F32)
        km_lo = km - km_hi - km_mid
        gate3 = jnp.dot(jnp.concatenate([km_hi, km_mid, km_lo], axis=0).astype(BF16), qm,
                        preferred_element_type=F32)
        gate = gate3[0:nb] + gate3[nb:2 * nb] + gate3[2 * nb:3 * nb]
        gate = jnp.where(past, gate, NEG_INF)
        rank = jnp.zeros((nb, blk), jnp.int32)
        for m in range(nb):
            gm = gate[m:m + 1, :]
            beats = (gm > gate) | ((gm == gate) & (m < nidx))
            rank = rank + beats.astype(jnp.int32)
        selected = past & (rank < MOBA_TOPK)
        rowbias_ref[h] = jnp.where(
            nidx == j, 0.0,
            jnp.where(selected, 0.0, NEG_INF) - slope * ((j - nidx) * blk + qcol).astype(F32))
        m_ref[h] = jnp.full((1, blk), NEG_INF, F32)
        l_ref[h] = jnp.zeros((1, blk), F32)
        acc_ref[h] = jnp.zeros((HEAD_DIM, blk), F32)

    def scores(n, h):
        s_ref[h] = jnp.dot(k_ref[0, n, :, lane_tile(h)], qm_ref[h], preferred_element_type=F32)

    def accumulate(n, own, h):
        t = s_ref[h] + bias_ref[own, h]
        rb = jnp.sum(jnp.where(nidx == n, rowbias_ref[h], 0.0), axis=0, keepdims=True)
        m = m_ref[h]
        m_new = jnp.maximum(m, _tree_max_rows(t) + rb)
        p = jnp.exp2(t - (m_new - rb))
        alpha = jnp.exp2(m - m_new)
        m_ref[h] = m_new
        lhs = jnp.concatenate([vt_ref[0, n, h * HEAD_DIM:(h + 1) * HEAD_DIM, :], ones], axis=0)
        pv = jnp.dot(lhs, p.astype(BF16), preferred_element_type=F32)
        l_ref[h] = alpha * l_ref[h] + pv[HEAD_DIM:HEAD_DIM + 1, :]
        acc_ref[h] = alpha * acc_ref[h] + pv[0:HEAD_DIM, :]

    for h in range(SCORE_LOOKAHEAD):
        scores(j, h)

    def body(i, carry):
        n = j - i
        n_next = jnp.maximum(n - 1, 0)
        own = (i == 0).astype(jnp.int32)
        for h in range(N_HEADS):
            if h + SCORE_LOOKAHEAD < N_HEADS:
                scores(n, h + SCORE_LOOKAHEAD)
            else:
                scores(n_next, h + SCORE_LOOKAHEAD - N_HEADS)
            accumulate(n, own, h)
        return carry

    lax.fori_loop(0, j + 1, body, 0)
    for t in range(N_HEADS // HEADS_PER_LANE_TILE):
        o_t = jnp.concatenate([acc_ref[h] / l_ref[h] for h in range(t * HEADS_PER_LANE_TILE,
                                                                     (t + 1) * HEADS_PER_LANE_TILE)], axis=0)
        o_ref[0, :, t * LANES:(t + 1) * LANES] = o_t.T.astype(o_ref.dtype)


def _moba_attention(slopes, qt, k4, vt, kmean):
    bsz, nb, _, blk = qt.shape
    return pl.pallas_call(
        _moba_kernel,
        grid=(bsz, nb),
        in_specs=[
            pl.BlockSpec(memory_space=pltpu.SMEM),
            pl.BlockSpec((1, 1, D_ATTN, blk), lambda b, j: (b, j, 0, 0)),
            pl.BlockSpec((1, nb, blk, D_ATTN), lambda b, j: (b, 0, 0, 0)),
            pl.BlockSpec((1, nb, D_ATTN, blk), lambda b, j: (b, 0, 0, 0)),
            pl.BlockSpec((1, nb, D_ATTN), lambda b, j: (b, 0, 0)),
        ],
        out_specs=pl.BlockSpec((1, blk, D_ATTN), lambda b, j: (b, j, 0)),
        out_shape=jax.ShapeDtypeStruct((bsz, nb * blk, D_ATTN), BF16),
        scratch_shapes=[
            pltpu.VMEM((N_HEADS, LANES, blk), BF16),
            pltpu.VMEM((2, N_HEADS, blk, blk), F32),
            pltpu.VMEM((N_HEADS, nb, blk), F32),
            pltpu.VMEM((N_HEADS, blk, blk), F32),
            pltpu.VMEM((N_HEADS, 1, blk), F32),
            pltpu.VMEM((N_HEADS, 1, blk), F32),
            pltpu.VMEM((N_HEADS, HEAD_DIM, blk), F32),
        ],
        compiler_params=pltpu.CompilerParams(
            dimension_semantics=("arbitrary", "arbitrary"), vmem_limit_bytes=ATTN_VMEM_LIMIT_BYTES),
        name="moba_attn",
    )(slopes, qt, k4, vt, kmean)


def _s5_kernel(u_ref, wb_ref, are_ref, aim_ref, wc_ref, d_ref, y_ref, ubuf, ybuf, xbuf, hstate):
    n_tiles, bsz, steps, _ = u_ref.shape
    rows = bsz * steps
    u2 = u_ref.reshape(n_tiles, rows, LANES)
    y2 = y_ref.reshape(n_tiles, rows, LANES)

    @pl.when(pl.program_id(0) == 0)
    def _():
        hstate[...] = jnp.zeros_like(hstate)

    for lt in range(n_tiles):
        for t in range(steps):
            ubuf[t * bsz:(t + 1) * bsz, lt * LANES:(lt + 1) * LANES] = u2[lt, pl.ds(t, bsz, stride=steps), :]

    def state_cols(lt):
        return slice(lt * STATE_COLS_PER_LANE_TILE, (lt + 1) * STATE_COLS_PER_LANE_TILE)

    def project_in(lt):
        u_t = ubuf[:, lt * LANES:(lt + 1) * LANES].astype(BF16)
        xbuf[:, state_cols(lt)] = jnp.dot(u_t, wb_ref[lt], preferred_element_type=F32)

    def scan(lt):
        pairs = range(lt * PAIRS_PER_LANE_TILE, (lt + 1) * PAIRS_PER_LANE_TILE)
        a_re = [jnp.broadcast_to(are_ref[pi:pi + 1, :], (SUBLANES, LANES)) for pi in pairs]
        a_im = [jnp.broadcast_to(aim_ref[pi:pi + 1, :], (SUBLANES, LANES)) for pi in pairs]
        cols = [pi * 2 * LANES for pi in pairs]
        hs = [(hstate[:, c:c + LANES], hstate[:, c + LANES:c + 2 * LANES]) for c in cols]
        for t in range(steps):
            r = slice(t * SUBLANES, (t + 1) * SUBLANES)
            for i, (c, ar, ai) in enumerate(zip(cols, a_re, a_im)):
                hr, hi = hs[i]
                nr = ar * hr - ai * hi + xbuf[r, c:c + LANES]
                ni = ar * hi + ai * hr + xbuf[r, c + LANES:c + 2 * LANES]
                xbuf[r, c:c + LANES] = nr
                xbuf[r, c + LANES:c + 2 * LANES] = ni
                hs[i] = (nr, ni)
        for c, (hr, hi) in zip(cols, hs):
            hstate[:, c:c + LANES] = hr
            hstate[:, c + LANES:c + 2 * LANES] = hi

    def project_out(lt):
        h_t = xbuf[:, state_cols(lt)].astype(BF16)
        lanes = slice(lt * LANES, (lt + 1) * LANES)
        ybuf[:, lanes] = (jnp.dot(h_t, wc_ref[lt], preferred_element_type=F32)
                          + d_ref[:, lanes] * ubuf[:, lanes])

    project_in(0)
    for lt in range(n_tiles):
        if lt + 1 < n_tiles:
            project_in(lt + 1)
        scan(lt)
        project_out(lt)
    for lt in range(n_tiles):
        for t in range(steps):
            y2[lt, pl.ds(t, bsz, stride=steps), :] = ybuf[t * bsz:(t + 1) * bsz, lt * LANES:(lt + 1) * LANES]


def _s5_weights(lam_re, lam_im, log_dt, b_re, b_im, c_re, c_im):
    g, p = lam_re.shape
    n_tiles = g // GROUPS_PER_LANE_TILE
    dt = jnp.exp(log_dt)[:, None]
    mag = jnp.exp(lam_re * dt)
    ang = lam_im * dt
    a_re = mag * jnp.cos(ang)
    a_im = mag * jnp.sin(ang)
    den = lam_re * lam_re + lam_im * lam_im
    f_re = ((a_re - 1.0) * lam_re + a_im * lam_im) / den
    f_im = (a_im * lam_re - (a_re - 1.0) * lam_im) / den
    bb_re = f_re[..., None] * b_re - f_im[..., None] * b_im
    bb_im = f_re[..., None] * b_im + f_im[..., None] * b_re
    gl = np.arange(GROUPS_PER_LANE_TILE)[:, None, None]
    delta = jnp.asarray(gl == 2 * np.arange(PAIRS_PER_LANE_TILE)[None, :, None] + np.arange(2)[None, None, :], F32)
    bbs = jnp.stack([bb_re, bb_im]).reshape(2, n_tiles, GROUPS_PER_LANE_TILE, p, SSM_GROUP)
    wb = jnp.einsum('rlgpi,gkj->lgikrjp', bbs, delta).reshape(n_tiles, LANES, STATE_COLS_PER_LANE_TILE)
    cs = jnp.stack([c_re, -c_im]).reshape(2, n_tiles, GROUPS_PER_LANE_TILE, SSM_GROUP, p)
    wc = jnp.einsum('rlgop,gkj->lkrjpgo', cs, delta).reshape(n_tiles, STATE_COLS_PER_LANE_TILE, LANES)
    n_pairs = g // 2
    return (wb.astype(BF16), a_re.reshape(n_pairs, 2 * p), a_im.reshape(n_pairs, 2 * p), wc.astype(BF16))


def _s5_scan(us, wb, a_re, a_im, wc, d_skip):
    n_tiles, bsz, seq, _ = us.shape
    d_ssm = n_tiles * LANES
    steps = SCAN_STEPS
    rows = steps * bsz
    state_cols = n_tiles * STATE_COLS_PER_LANE_TILE
    const = lambda shape: pl.BlockSpec(shape, lambda c: (0,) * len(shape))
    return pl.pallas_call(
        _s5_kernel,
        grid=(seq // steps,),
        in_specs=[
            pl.BlockSpec((n_tiles, bsz, steps, LANES), lambda c: (0, 0, c, 0)),
            const(wb.shape), const(a_re.shape), const(a_im.shape), const(wc.shape), const((1, d_ssm)),
        ],
        out_specs=pl.BlockSpec((n_tiles, bsz, steps, LANES), lambda c: (0, 0, c, 0)),
        out_shape=jax.ShapeDtypeStruct((n_tiles, bsz, seq, LANES), F32),
        scratch_shapes=[pltpu.VMEM((rows, d_ssm), F32), pltpu.VMEM((rows, d_ssm), F32),
                        pltpu.VMEM((rows, state_cols), F32), pltpu.VMEM((bsz, state_cols), F32)],
        compiler_params=pltpu.CompilerParams(
            dimension_semantics=("arbitrary",), vmem_limit_bytes=VMEM_LIMIT_BYTES),
        name="s5_scan",
    )(us, wb, a_re, a_im, wc, d_skip)


def _mix_ffn2_kernel(h_ref, attn_ref, y_ref, wglu, bglu, attn_g, ssm_g, wout, mixpost_g,
                     pre_g, wg, wu, wd, post_g, o_ref, act_ref):
    y = jnp.concatenate([y_ref[lt] for lt in range(y_ref.shape[0])], axis=-1)
    y = jax.nn.gelu(y, approximate=True)
    z = y * jax.nn.sigmoid(jnp.dot(y.astype(BF16), wglu[...], preferred_element_type=F32) + bglu[...])
    zs = _rms(z, ssm_g[...]).astype(BF16)
    at = _rms(attn_ref[...].astype(F32), attn_g[...]).astype(BF16)
    mixed = (jnp.dot(at, wout[0:D_ATTN, :], preferred_element_type=F32)
             + jnp.dot(zs, wout[D_ATTN:, :], preferred_element_type=F32))
    h = h_ref[...] + _rms(mixed, mixpost_g[...])
    f = _swiglu(_rms(h, pre_g[...]).astype(BF16), wg, wu, wd, act_ref)
    o_ref[...] = h + 0.5 * _rms(f, post_g[...])


def _mix_ffn2(h2, attn2, y2, wglu, bglu, attn_g, ssm_g, wout, mixpost_g, pre_g, wg, wu, wd, post_g, seq):
    n, d = h2.shape
    d_ff = wg.shape[1]
    d_ssm = wglu.shape[0]
    tm = ROW_TILE
    tiles_per_seq = seq // tm
    const = lambda shape: pl.BlockSpec(shape, lambda i: (0,) * len(shape), pipeline_mode=pl.Buffered(1))
    return pl.pallas_call(
        _mix_ffn2_kernel,
        grid=(n // tm,),
        in_specs=[
            pl.BlockSpec((tm, d), lambda i: (i, 0)),
            pl.BlockSpec((tm, D_ATTN), lambda i: (i, 0)),
            pl.BlockSpec((d_ssm // LANES, tm, LANES), lambda i: (0, i, 0)),
            const((d_ssm, d_ssm)), const((1, d_ssm)), const((1, D_ATTN)), const((1, d_ssm)),
            const((D_ATTN + d_ssm, d)), const((1, d)),
            const((1, d)), const((d, d_ff)), const((d, d_ff)), const((d_ff, d)), const((1, d)),
        ],
        out_specs=pl.BlockSpec((tm, d), lambda i: (i, 0)),
        out_shape=jax.ShapeDtypeStruct((n, d), F32),
        scratch_shapes=[pltpu.VMEM((tm, d_ff), BF16)],
        compiler_params=pltpu.CompilerParams(
            dimension_semantics=("arbitrary",), vmem_limit_bytes=VMEM_LIMIT_BYTES),
        name="mix_ffn2",
    )(h2, attn2, y2, wglu, bglu, attn_g, ssm_g, wout, mixpost_g, pre_g, wg, wu, wd, post_g)


def kernel(x, ffn1_pre_g, ffn1_w_gate, ffn1_w_up, ffn1_w_down, ffn1_post_g, mix_pre_g, w_in, lam_re, lam_im, log_dt, b_re, b_im, c_re, c_im, d_skip, w_glu, b_glu, attn_out_g, ssm_out_g, w_out, mix_post_g, ffn2_pre_g, ffn2_w_gate, ffn2_w_up, ffn2_w_down, ffn2_post_g):
    bsz, seq, d = x.shape
    depth = w_in.shape[0]
    d_ssm = w_glu.shape[1]
    assert w_in.shape[2] == 3 * D_ATTN + d_ssm and d_ssm == lam_re.shape[1] * SSM_GROUP
    assert seq % ROW_TILE == 0 and ROW_TILE % MOBA_BLOCK == 0 and seq % SCAN_STEPS == 0
    assert bsz == SUBLANES and lam_re.shape[2] == SSM_STATE
    nb = seq // MOBA_BLOCK
    slopes = jnp.asarray(LOG2E * 2.0 ** (-8.0 * np.arange(1, N_HEADS + 1) / N_HEADS), dtype=F32)
    row = lambda g: g.reshape(1, -1).astype(F32)
    bf = lambda w: w.astype(BF16)

    h2 = x.reshape(bsz * seq, d)
    for l in range(depth):
        h2, qt, k2, vt, us, kmean = _ffn1_proj(
            h2, row(ffn1_pre_g[l]), bf(ffn1_w_gate[l]), bf(ffn1_w_up[l]), bf(ffn1_w_down[l]),
            row(ffn1_post_g[l]), row(mix_pre_g[l]), bf(w_in[l]), bsz, seq)
        attn = _moba_attention(slopes, qt, k2.reshape(bsz, nb, MOBA_BLOCK, D_ATTN), vt,
                               kmean.reshape(bsz, nb, D_ATTN))
        wb, a_re, a_im, wc = _s5_weights(lam_re[l], lam_im[l], log_dt[l], b_re[l], b_im[l], c_re[l], c_im[l])
        y = _s5_scan(us.reshape(-1, bsz, seq, LANES), wb, a_re, a_im, wc, row(d_skip[l]))
        h2 = _mix_ffn2(
            h2, attn.reshape(bsz * seq, D_ATTN), y.reshape(-1, bsz * seq, LANES),
            bf(w_glu[l]), row(b_glu[l]), row(attn_out_g[l]), row(ssm_out_g[l]), bf(w_out[l]),
            row(mix_post_g[l]), row(ffn2_pre_g[l]), bf(ffn2_w_gate[l]), bf(ffn2_w_up[l]),
            bf(ffn2_w_down[l]), row(ffn2_post_g[l]), seq)
    return h2.reshape(bsz, seq, d)
```

```python
import functools
import math

import numpy as np
import jax
import jax.numpy as jnp
from jax import lax
from jax.experimental import pallas as pl
from jax.experimental.pallas import tpu as pltpu

F32 = jnp.float32
BF16 = jnp.bfloat16

N_HEADS = 8
HEAD_DIM = 64
D_ATTN = N_HEADS * HEAD_DIM
MOBA_BLOCK = 256
MOBA_TOPK = 3
SSM_GROUP = 16
SSM_STATE = 64
RMS_EPS = 1e-6
NEG_INF = -1e30

LANES = 128
SUBLANES = 8
BF16_SUBLANES = 16
HEADS_PER_LANE_TILE = LANES // HEAD_DIM
GROUPS_PER_LANE_TILE = LANES // SSM_GROUP
PAIRS_PER_LANE_TILE = GROUPS_PER_LANE_TILE // 2
STATE_COLS_PER_LANE_TILE = GROUPS_PER_LANE_TILE * 2 * SSM_STATE

ROW_TILE = 512
SCAN_STEPS = 64
SCORE_LOOKAHEAD = 4
VMEM_LIMIT_BYTES = 56 * 1024 * 1024
ATTN_VMEM_LIMIT_BYTES = 32 * 1024 * 1024
LOG2E = math.log2(math.e)


def _rms(x, g):
    return x * lax.rsqrt(jnp.mean(x * x, axis=-1, keepdims=True) + RMS_EPS) * g


def _ff_chunks(d_ff, width=768):
    return [(s, min(width, d_ff - s)) for s in range(0, d_ff, width)]


def _swiglu(xn, wg_ref, wu_ref, wd_ref, act_ref):
    for s, w in _ff_chunks(wg_ref.shape[1]):
        g = jnp.dot(xn, wg_ref[:, s:s + w], preferred_element_type=F32)
        u = jnp.dot(xn, wu_ref[:, s:s + w], preferred_element_type=F32)
        act_ref[:, s:s + w] = (g * jax.nn.sigmoid(g) * u).astype(BF16)
    return jnp.dot(act_ref[...], wd_ref[...], preferred_element_type=F32)


def _ffn1_proj_kernel(x_ref, pre_g, wg, wu, wd, post_g, mix_g, win,
                      h_ref, qt_ref, k_ref, vt_ref, us_ref, kmean_ref, act_ref):
    x = x_ref[...]
    f = _swiglu(_rms(x, pre_g[...]).astype(BF16), wg, wu, wd, act_ref)
    h = x + 0.5 * _rms(f, post_g[...])
    h_ref[...] = h
    u = _rms(h, mix_g[...]).astype(BF16)
    q = jnp.dot(u, win[:, 0:D_ATTN], preferred_element_type=F32) * (LOG2E / math.sqrt(HEAD_DIM))
    k = jnp.dot(u, win[:, D_ATTN:2 * D_ATTN], preferred_element_type=F32)
    v = jnp.dot(u, win[:, 2 * D_ATTN:3 * D_ATTN], preferred_element_type=F32)
    us = jnp.dot(u, win[:, 3 * D_ATTN:], preferred_element_type=F32)
    for lt in range(us_ref.shape[0]):
        us_ref[lt] = us[:, lt * LANES:(lt + 1) * LANES]
    k_ref[...] = k.astype(BF16)
    for blk in range(x.shape[0] // MOBA_BLOCK):
        rows = slice(blk * MOBA_BLOCK, (blk + 1) * MOBA_BLOCK)
        qt_ref[0, blk] = q[rows].T.astype(BF16)
        vt_ref[0, blk] = v[rows].T.astype(BF16)
        kmean_ref[0, blk:blk + 1, :] = jnp.mean(k[rows], axis=0, keepdims=True)


def _ffn1_proj(x2, pre_g, wg, wu, wd, post_g, mix_g, win, bsz, seq):
    n, d = x2.shape
    d_ff = wg.shape[1]
    d_in = win.shape[1]
    d_ssm = d_in - 3 * D_ATTN
    tm = ROW_TILE
    tiles_per_seq = seq // tm
    blocks_per_tile = tm // MOBA_BLOCK
    nb = seq // MOBA_BLOCK
    const = lambda shape: pl.BlockSpec(shape, lambda i: (0,) * len(shape), pipeline_mode=pl.Buffered(1))
    return pl.pallas_call(
        _ffn1_proj_kernel,
        grid=(n // tm,),
        in_specs=[
            pl.BlockSpec((tm, d), lambda i: (i, 0)),
            const((1, d)), const((d, d_ff)), const((d, d_ff)), const((d_ff, d)), const((1, d)),
            const((1, d)), const((d, d_in)),
        ],
        out_specs=[
            pl.BlockSpec((tm, d), lambda i: (i, 0)),
            pl.BlockSpec((1, blocks_per_tile, D_ATTN, MOBA_BLOCK),
                         lambda i: (i // tiles_per_seq, i % tiles_per_seq, 0, 0)),
            pl.BlockSpec((tm, D_ATTN), lambda i: (i, 0)),
            pl.BlockSpec((1, blocks_per_tile, D_ATTN, MOBA_BLOCK),
                         lambda i: (i // tiles_per_seq, i % tiles_per_seq, 0, 0)),
            pl.BlockSpec((d_ssm // LANES, tm, LANES), lambda i: (0, i, 0)),
            pl.BlockSpec((1, blocks_per_tile, D_ATTN), lambda i: (i, 0, 0)),
        ],
        out_shape=[
            jax.ShapeDtypeStruct((n, d), F32),
            jax.ShapeDtypeStruct((bsz, nb, D_ATTN, MOBA_BLOCK), BF16),
            jax.ShapeDtypeStruct((n, D_ATTN), BF16),
            jax.ShapeDtypeStruct((bsz, nb, D_ATTN, MOBA_BLOCK), BF16),
            jax.ShapeDtypeStruct((d_ssm // LANES, n, LANES), F32),
            jax.ShapeDtypeStruct((n // tm, blocks_per_tile, D_ATTN), F32),
        ],
        scratch_shapes=[pltpu.VMEM((tm, d_ff), BF16)],
        compiler_params=pltpu.CompilerParams(
            dimension_semantics=("arbitrary",), vmem_limit_bytes=VMEM_LIMIT_BYTES),
        name="ffn1_proj",
    )(x2, pre_g, wg, wu, wd, post_g, mix_g, win)


def _tree_max_rows(t):
    while t.shape[0] > SUBLANES:
        half = t.shape[0] // 2
        t = jnp.maximum(t[:half], t[half:])
    return jnp.max(t, axis=0, keepdims=True)


def _moba_kernel(slopes_ref, qt_ref, k_ref, vt_ref, kmean_ref, *rest):
    n_cast = (len(rest) - 8) // 2
    cast_in, o_ref, cast_out = rest[:n_cast], rest[n_cast], rest[n_cast + 1:2 * n_cast + 1]
    qm_ref, bias_ref, rowbias_ref, s_ref, m_ref, l_ref, acc_ref = rest[2 * n_cast + 1:]
    for w_in_ref, w_out_ref in zip(cast_in, cast_out):
        w_out_ref[...] = w_in_ref[...].astype(w_out_ref.dtype)
    j = pl.program_id(1)
    blk = MOBA_BLOCK
    nb = k_ref.shape[1]
    kpos = lax.broadcasted_iota(jnp.int32, (blk, blk), 0)
    qpos = lax.broadcasted_iota(jnp.int32, (blk, blk), 1)
    nidx = lax.broadcasted_iota(jnp.int32, (nb, blk), 0)
    qcol = lax.broadcasted_iota(jnp.int32, (nb, blk), 1)
    chan = lax.broadcasted_iota(jnp.int32, (LANES, blk), 0)
    past = nidx < j
    ones = jnp.ones((BF16_SUBLANES, blk), BF16)

    @pl.when((pl.program_id(0) == 0) & (j == 0))
    def _():
        for h in range(N_HEADS):
            slope = slopes_ref[h]
            bias_ref[0, h] = slope * kpos.astype(F32)
            bias_ref[1, h] = jnp.where(qpos >= kpos, slope * (kpos - qpos).astype(F32), NEG_INF)

    def lane_tile(h):
        t = h // HEADS_PER_LANE_TILE
        return slice(t * LANES, (t + 1) * LANES)

    for h in range(N_HEADS):
        slope = slopes_ref[h]
        lanes = lane_tile(h)
        lo = (h % HEADS_PER_LANE_TILE) * HEAD_DIM
        qt2 = qt_ref[0, 0, lanes, :]
        qm = jnp.where((chan >= lo) & (chan < lo + HEAD_DIM), qt2, jnp.zeros_like(qt2))
        qm_ref[h] = qm
        km = kmean_ref[0, :, lanes]
        km_hi = km.astype(BF16).astype(F32)
        km_mid = (km - km_hi).astype(BF16).astype(F32)
        km_lo = km - km_hi - km_mid
        gate3 = jnp.dot(jnp.concatenate([km_hi, km_mid, km_lo], axis=0).astype(BF16), qm,
                        preferred_element_type=F32)
        gate = gate3[0:nb] + gate3[nb:2 * nb] + gate3[2 * nb:3 * nb]
        gate = jnp.where(past, gate, NEG_INF)
        rank = jnp.zeros((nb, blk), jnp.int32)
        for m in range(nb):
            gm = gate[m:m + 1, :]
            beats = (gm > gate) | ((gm == gate) & (m < nidx))
            rank = rank + beats.astype(jnp.int32)
        selected = past & (rank < MOBA_TOPK)
        rowbias_ref[h] = jnp.where(
            nidx == j, 0.0,
            jnp.where(selected, 0.0, NEG_INF) - slope * ((j - nidx) * blk + qcol).astype(F32))
        m_ref[h] = jnp.full((1, blk), NEG_INF, F32)
        l_ref[h] = jnp.zeros((1, blk), F32)
        acc_ref[h] = jnp.zeros((HEAD_DIM, blk), F32)

    def scores(n, h):
        s_ref[h] = jnp.dot(k_ref[0, n, :, lane_tile(h)], qm_ref[h], preferred_element_type=F32)

    def accumulate(n, own, h):
        t = s_ref[h] + bias_ref[own, h]
        rb = jnp.sum(jnp.where(nidx == n, rowbias_ref[h], 0.0), axis=0, keepdims=True)
        m = m_ref[h]
        m_new = jnp.maximum(m, _tree_max_rows(t) + rb)
        p = jnp.exp2(t - (m_new - rb))
        alpha = jnp.exp2(m - m_new)
        m_ref[h] = m_new
        lhs = jnp.concatenate([vt_ref[0, n, h * HEAD_DIM:(h + 1) * HEAD_DIM, :], ones], axis=0)
        pv = jnp.dot(lhs, p.astype(BF16), preferred_element_type=F32)
        l_ref[h] = alpha * l_ref[h] + pv[HEAD_DIM:HEAD_DIM + 1, :]
        acc_ref[h] = alpha * acc_ref[h] + pv[0:HEAD_DIM, :]

    for h in range(SCORE_LOOKAHEAD):
        scores(j, h)

    def body(i, carry):
        n = j - i
        n_next = jnp.maximum(n - 1, 0)
        own = (i == 0).astype(jnp.int32)
        for h in range(N_HEADS):
            if h + SCORE_LOOKAHEAD < N_HEADS:
                scores(n, h + SCORE_LOOKAHEAD)
            else:
                scores(n_next, h + SCORE_LOOKAHEAD - N_HEADS)
            accumulate(n, own, h)
        return carry

    lax.fori_loop(0, j + 1, body, 0)
    for t in range(N_HEADS // HEADS_PER_LANE_TILE):
        o_t = jnp.concatenate([acc_ref[h] / l_ref[h] for h in range(t * HEADS_PER_LANE_TILE,
                                                                     (t + 1) * HEADS_PER_LANE_TILE)], axis=0)
        o_ref[0, :, t * LANES:(t + 1) * LANES] = o_t.T.astype(o_ref.dtype)


def _cast_spec(rows, cols, steps, nb):
    n_blocks = next(n for n in (steps >> s for s in range(steps.bit_length()))
                    if steps % n == 0 and rows % n == 0 and (rows // n) % BF16_SUBLANES == 0)
    steps_per_block = steps // n_blocks
    return pl.BlockSpec((rows // n_blocks, cols), lambda b, j: ((b * nb + j) // steps_per_block, 0))


def _moba_attention(slopes, qt, k4, vt, kmean, cast_weights):
    bsz, nb, _, blk = qt.shape
    cast_specs = [_cast_spec(w.shape[0], w.shape[1], bsz * nb, nb) for w in cast_weights]
    return pl.pallas_call(
        _moba_kernel,
        grid=(bsz, nb),
        in_specs=[
            pl.BlockSpec(memory_space=pltpu.SMEM),
            pl.BlockSpec((1, 1, D_ATTN, blk), lambda b, j: (b, j, 0, 0)),
            pl.BlockSpec((1, nb, blk, D_ATTN), lambda b, j: (b, 0, 0, 0)),
            pl.BlockSpec((1, nb, D_ATTN, blk), lambda b, j: (b, 0, 0, 0)),
            pl.BlockSpec((1, nb, D_ATTN), lambda b, j: (b, 0, 0)),
        ] + cast_specs,
        out_specs=[pl.BlockSpec((1, blk, D_ATTN), lambda b, j: (b, j, 0))] + cast_specs,
        out_shape=[jax.ShapeDtypeStruct((bsz, nb * blk, D_ATTN), BF16)]
        + [jax.ShapeDtypeStruct(w.shape, BF16) for w in cast_weights],
        scratch_shapes=[
            pltpu.VMEM((N_HEADS, LANES, blk), BF16),
            pltpu.VMEM((2, N_HEADS, blk, blk), F32),
            pltpu.VMEM((N_HEADS, nb, blk), F32),
            pltpu.VMEM((N_HEADS, blk, blk), F32),
            pltpu.VMEM((N_HEADS, 1, blk), F32),
            pltpu.VMEM((N_HEADS, 1, blk), F32),
            pltpu.VMEM((N_HEADS, HEAD_DIM, blk), F32),
        ],
        compiler_params=pltpu.CompilerParams(
            dimension_semantics=("arbitrary", "arbitrary"), vmem_limit_bytes=ATTN_VMEM_LIMIT_BYTES),
        name="moba_attn",
    )(slopes, qt, k4, vt, kmean, *cast_weights)


def _s5_kernel(u_ref, wb_ref, are_ref, aim_ref, wc_ref, d_ref, y_ref, ubuf, ybuf, xbuf, hstate):
    n_tiles, bsz, steps, _ = u_ref.shape
    rows = bsz * steps
    u2 = u_ref.reshape(n_tiles, rows, LANES)
    y2 = y_ref.reshape(n_tiles, rows, LANES)

    @pl.when(pl.program_id(0) == 0)
    def _():
        hstate[...] = jnp.zeros_like(hstate)

    for lt in range(n_tiles):
        for t in range(steps):
            ubuf[t * bsz:(t + 1) * bsz, lt * LANES:(lt + 1) * LANES] = u2[lt, pl.ds(t, bsz, stride=steps), :]

    def state_cols(lt):
        return slice(lt * STATE_COLS_PER_LANE_TILE, (lt + 1) * STATE_COLS_PER_LANE_TILE)

    def project_in(lt):
        u_t = ubuf[:, lt * LANES:(lt + 1) * LANES].astype(BF16)
        xbuf[:, state_cols(lt)] = jnp.dot(u_t, wb_ref[lt], preferred_element_type=F32)

    def scan(lt):
        pairs = range(lt * PAIRS_PER_LANE_TILE, (lt + 1) * PAIRS_PER_LANE_TILE)
        a_re = [jnp.broadcast_to(are_ref[pi:pi + 1, :], (SUBLANES, LANES)) for pi in pairs]
        a_im = [jnp.broadcast_to(aim_ref[pi:pi + 1, :], (SUBLANES, LANES)) for pi in pairs]
        cols = [pi * 2 * LANES for pi in pairs]
        hs = [(hstate[:, c:c + LANES], hstate[:, c + LANES:c + 2 * LANES]) for c in cols]
        for t in range(steps):
            r = slice(t * SUBLANES, (t + 1) * SUBLANES)
            for i, (c, ar, ai) in enumerate(zip(cols, a_re, a_im)):
                hr, hi = hs[i]
                nr = ar * hr - ai * hi + xbuf[r, c:c + LANES]
                ni = ar * hi + ai * hr + xbuf[r, c + LANES:c + 2 * LANES]
                xbuf[r, c:c + LANES] = nr
                xbuf[r, c + LANES:c + 2 * LANES] = ni
                hs[i] = (nr, ni)
        for c, (hr, hi) in zip(cols, hs):
            hstate[:, c:c + LANES] = hr
            hstate[:, c + LANES:c + 2 * LANES] = hi

    def project_out(lt):
        h_t = xbuf[:, state_cols(lt)].astype(BF16)
        lanes = slice(lt * LANES, (lt + 1) * LANES)
        ybuf[:, lanes] = (jnp.dot(h_t, wc_ref[lt], preferred_element_type=F32)
                          + d_ref[:, lanes] * ubuf[:, lanes])

    project_in(0)
    for lt in range(n_tiles):
        if lt + 1 < n_tiles:
            project_in(lt + 1)
        scan(lt)
        project_out(lt)
    for lt in range(n_tiles):
        for t in range(steps):
            y2[lt, pl.ds(t, bsz, stride=steps), :] = ybuf[t * bsz:(t + 1) * bsz, lt * LANES:(lt + 1) * LANES]


def _s5_weights(lam_re, lam_im, log_dt, b_re, b_im, c_re, c_im):
    g, p = lam_re.shape
    n_tiles = g // GROUPS_PER_LANE_TILE
    dt = jnp.exp(log_dt)[:, None]
    mag = jnp.exp(lam_re * dt)
    ang = lam_im * dt
    a_re = mag * jnp.cos(ang)
    a_im = mag * jnp.sin(ang)
    den = lam_re * lam_re + lam_im * lam_im
    f_re = ((a_re - 1.0) * lam_re + a_im * lam_im) / den
    f_im = (a_im * lam_re - (a_re - 1.0) * lam_im) / den
    bb_re = f_re[..., None] * b_re - f_im[..., None] * b_im
    bb_im = f_re[..., None] * b_im + f_im[..., None] * b_re
    gl = np.arange(GROUPS_PER_LANE_TILE)[:, None, None]
    delta = jnp.asarray(gl == 2 * np.arange(PAIRS_PER_LANE_TILE)[None, :, None] + np.arange(2)[None, None, :], F32)
    bbs = jnp.stack([bb_re, bb_im]).reshape(2, n_tiles, GROUPS_PER_LANE_TILE, p, SSM_GROUP)
    wb = jnp.einsum('rlgpi,gkj->lgikrjp', bbs, delta).reshape(n_tiles, LANES, STATE_COLS_PER_LANE_TILE)
    cs = jnp.stack([c_re, -c_im]).reshape(2, n_tiles, GROUPS_PER_LANE_TILE, SSM_GROUP, p)
    wc = jnp.einsum('rlgop,gkj->lkrjpgo', cs, delta).reshape(n_tiles, STATE_COLS_PER_LANE_TILE, LANES)
    n_pairs = g // 2
    return (wb.astype(BF16), a_re.reshape(n_pairs, 2 * p), a_im.reshape(n_pairs, 2 * p), wc.astype(BF16))


def _s5_scan(us, wb, a_re, a_im, wc, d_skip):
    n_tiles, bsz, seq, _ = us.shape
    d_ssm = n_tiles * LANES
    steps = SCAN_STEPS
    rows = steps * bsz
    state_cols = n_tiles * STATE_COLS_PER_LANE_TILE
    const = lambda shape: pl.BlockSpec(shape, lambda c: (0,) * len(shape))
    return pl.pallas_call(
        _s5_kernel,
        grid=(seq // steps,),
        in_specs=[
            pl.BlockSpec((n_tiles, bsz, steps, LANES), lambda c: (0, 0, c, 0)),
            const(wb.shape), const(a_re.shape), const(a_im.shape), const(wc.shape), const((1, d_ssm)),
        ],
        out_specs=pl.BlockSpec((n_tiles, bsz, steps, LANES), lambda c: (0, 0, c, 0)),
        out_shape=jax.ShapeDtypeStruct((n_tiles, bsz, seq, LANES), F32),
        scratch_shapes=[pltpu.VMEM((rows, d_ssm), F32), pltpu.VMEM((rows, d_ssm), F32),
                        pltpu.VMEM((rows, state_cols), F32), pltpu.VMEM((bsz, state_cols), F32)],
        compiler_params=pltpu.CompilerParams(
            dimension_semantics=("arbitrary",), vmem_limit_bytes=VMEM_LIMIT_BYTES),
        name="s5_scan",
    )(us, wb, a_re, a_im, wc, d_skip)


def _mix_ffn2_kernel(h_ref, attn_ref, y_ref, wglu, bglu, attn_g, ssm_g, wout, mixpost_g,
                     pre_g, wg, wu, wd, post_g, o_ref, act_ref):
    y = jnp.concatenate([y_ref[lt] for lt in range(y_ref.shape[0])], axis=-1)
    y = jax.nn.gelu(y, approximate=True)
    z = y * jax.nn.sigmoid(jnp.dot(y.astype(BF16), wglu[...], preferred_element_type=F32) + bglu[...])
    zs = _rms(z, ssm_g[...]).astype(BF16)
    at = _rms(attn_ref[...].astype(F32), attn_g[...]).astype(BF16)
    mixed = (jnp.dot(at, wout[0:D_ATTN, :], preferred_element_type=F32)
             + jnp.dot(zs, wout[D_ATTN:, :], preferred_element_type=F32))
    h = h_ref[...] + _rms(mixed, mixpost_g[...])
    f = _swiglu(_rms(h, pre_g[...]).astype(BF16), wg, wu, wd, act_ref)
    o_ref[...] = h + 0.5 * _rms(f, post_g[...])


def _mix_ffn2(h2, attn2, y2, wglu, bglu, attn_g, ssm_g, wout, mixpost_g, pre_g, wg, wu, wd, post_g, seq):
    n, d = h2.shape
    d_ff = wg.shape[1]
    d_ssm = wglu.shape[0]
    tm = ROW_TILE
    tiles_per_seq = seq // tm
    const = lambda shape: pl.BlockSpec(shape, lambda i: (0,) * len(shape), pipeline_mode=pl.Buffered(1))
    return pl.pallas_call(
        _mix_ffn2_kernel,
        grid=(n // tm,),
        in_specs=[
            pl.BlockSpec((tm, d), lambda i: (i, 0)),
            pl.BlockSpec((tm, D_ATTN), lambda i: (i, 0)),
            pl.BlockSpec((d_ssm // LANES, tm, LANES), lambda i: (0, i, 0)),
            const((d_ssm, d_ssm)), const((1, d_ssm)), const((1, D_ATTN)), const((1, d_ssm)),
            const((D_ATTN + d_ssm, d)), const((1, d)),
            const((1, d)), const((d, d_ff)), const((d, d_ff)), const((d_ff, d)), const((1, d)),
        ],
        out_specs=pl.BlockSpec((tm, d), lambda i: (i, 0)),
        out_shape=jax.ShapeDtypeStruct((n, d), F32),
        scratch_shapes=[pltpu.VMEM((tm, d_ff), BF16)],
        compiler_params=pltpu.CompilerParams(
            dimension_semantics=("arbitrary",), vmem_limit_bytes=VMEM_LIMIT_BYTES),
        name="mix_ffn2",
    )(h2, attn2, y2, wglu, bglu, attn_g, ssm_g, wout, mixpost_g, pre_g, wg, wu, wd, post_g)


def kernel(x, ffn1_pre_g, ffn1_w_gate, ffn1_w_up, ffn1_w_down, ffn1_post_g, mix_pre_g, w_in, lam_re, lam_im, log_dt, b_re, b_im, c_re, c_im, d_skip, w_glu, b_glu, attn_out_g, ssm_out_g, w_out, mix_post_g, ffn2_pre_g, ffn2_w_gate, ffn2_w_up, ffn2_w_down, ffn2_post_g):
    bsz, seq, d = x.shape
    depth = w_in.shape[0]
    d_ssm = w_glu.shape[1]
    assert w_in.shape[2] == 3 * D_ATTN + d_ssm and d_ssm == lam_re.shape[1] * SSM_GROUP
    assert seq % ROW_TILE == 0 and ROW_TILE % MOBA_BLOCK == 0 and seq % SCAN_STEPS == 0
    assert bsz == SUBLANES and lam_re.shape[2] == SSM_STATE
    nb = seq // MOBA_BLOCK
    slopes = jnp.asarray(LOG2E * 2.0 ** (-8.0 * np.arange(1, N_HEADS + 1) / N_HEADS), dtype=F32)
    row = lambda g: g.reshape(1, -1).astype(F32)
    bf = lambda w: w.astype(BF16)

    h2 = x.reshape(bsz * seq, d)
    for l in range(depth):
        h2, qt, k2, vt, us, kmean = _ffn1_proj(
            h2, row(ffn1_pre_g[l]), bf(ffn1_w_gate[l]), bf(ffn1_w_up[l]), bf(ffn1_w_down[l]),
            row(ffn1_post_g[l]), row(mix_pre_g[l]), bf(w_in[l]), bsz, seq)
        attn, wglu_bf, wout_bf, wg2_bf, wu2_bf, wd2_bf = _moba_attention(
            slopes, qt, k2.reshape(bsz, nb, MOBA_BLOCK, D_ATTN), vt, kmean.reshape(bsz, nb, D_ATTN),
            [w_glu[l], w_out[l], ffn2_w_gate[l], ffn2_w_up[l], ffn2_w_down[l]])
        wb, a_re, a_im, wc = _s5_weights(lam_re[l], lam_im[l], log_dt[l], b_re[l], b_im[l], c_re[l], c_im[l])
        y = _s5_scan(us.reshape(-1, bsz, seq, LANES), wb, a_re, a_im, wc, row(d_skip[l]))
        h2 = _mix_ffn2(
            h2, attn.reshape(bsz * seq, D_ATTN), y.reshape(-1, bsz * seq, LANES),
            wglu_bf, row(b_glu[l]), row(attn_out_g[l]), row(ssm_out_g[l]), wout_bf,
            row(mix_post_g[l]), row(ffn2_pre_g[l]), wg2_bf, wu2_bf, wd2_bf, row(ffn2_post_g[l]), seq)
    return h2.reshape(bsz, seq, d)
```

```python
import functools
import math

import numpy as np
import jax
import jax.numpy as jnp
from jax import lax
from jax.experimental import pallas as pl
from jax.experimental.pallas import tpu as pltpu

F32 = jnp.float32
BF16 = jnp.bfloat16

N_HEADS = 8
HEAD_DIM = 64
D_ATTN = N_HEADS * HEAD_DIM
MOBA_BLOCK = 256
MOBA_TOPK = 3
SSM_GROUP = 16
SSM_STATE = 64
RMS_EPS = 1e-6
NEG_INF = -1e30

LANES = 128
SUBLANES = 8
BF16_SUBLANES = 16
HEADS_PER_LANE_TILE = LANES // HEAD_DIM
GROUPS_PER_LANE_TILE = LANES // SSM_GROUP
PAIRS_PER_LANE_TILE = GROUPS_PER_LANE_TILE // 2
STATE_COLS_PER_LANE_TILE = GROUPS_PER_LANE_TILE * 2 * SSM_STATE

ROW_TILE = 512
SCAN_STEPS = 64
SCORE_LOOKAHEAD = 4
VMEM_LIMIT_BYTES = 56 * 1024 * 1024
ATTN_VMEM_LIMIT_BYTES = 32 * 1024 * 1024
LOG2E = math.log2(math.e)


def _rms(x, g):
    return x * lax.rsqrt(jnp.mean(x * x, axis=-1, keepdims=True) + RMS_EPS) * g


def _ff_chunks(d_ff, width=768):
    return [(s, min(width, d_ff - s)) for s in range(0, d_ff, width)]


def _swiglu(xn, wg_ref, wu_ref, wd_ref, act_ref):
    for s, w in _ff_chunks(wg_ref.shape[1]):
        g = jnp.dot(xn, wg_ref[:, s:s + w], preferred_element_type=F32)
        u = jnp.dot(xn, wu_ref[:, s:s + w], preferred_element_type=F32)
        act_ref[:, s:s + w] = (g * jax.nn.sigmoid(g) * u).astype(BF16)
    return jnp.dot(act_ref[...], wd_ref[...], preferred_element_type=F32)


def _ffn1_proj_kernel(x_ref, pre_g, wg, wu, wd, post_g, mix_g, win,
                      h_ref, qt_ref, k_ref, vt_ref, us_ref, kmean_ref, act_ref):
    x = x_ref[...]
    f = _swiglu(_rms(x, pre_g[...]).astype(BF16), wg, wu, wd, act_ref)
    h = x + 0.5 * _rms(f, post_g[...])
    h_ref[...] = h
    u = _rms(h, mix_g[...]).astype(BF16)
    q = jnp.dot(u, win[:, 0:D_ATTN], preferred_element_type=F32) * (LOG2E / math.sqrt(HEAD_DIM))
    k = jnp.dot(u, win[:, D_ATTN:2 * D_ATTN], preferred_element_type=F32)
    v = jnp.dot(u, win[:, 2 * D_ATTN:3 * D_ATTN], preferred_element_type=F32)
    us = jnp.dot(u, win[:, 3 * D_ATTN:], preferred_element_type=F32)
    for lt in range(us_ref.shape[0]):
        us_ref[lt] = us[:, lt * LANES:(lt + 1) * LANES]
    k_ref[...] = k.astype(BF16)
    for blk in range(x.shape[0] // MOBA_BLOCK):
        rows = slice(blk * MOBA_BLOCK, (blk + 1) * MOBA_BLOCK)
        qt_ref[0, blk] = q[rows].T.astype(BF16)
        vt_ref[0, blk] = v[rows].T.astype(BF16)
        kmean_ref[0, blk:blk + 1, :] = jnp.mean(k[rows], axis=0, keepdims=True)


def _ffn1_proj(x2, pre_g, wg, wu, wd, post_g, mix_g, win, bsz, seq):
    n, d = x2.shape
    d_ff = wg.shape[1]
    d_in = win.shape[1]
    d_ssm = d_in - 3 * D_ATTN
    tm = ROW_TILE
    tiles_per_seq = seq // tm
    blocks_per_tile = tm // MOBA_BLOCK
    nb = seq // MOBA_BLOCK
    const = lambda shape: pl.BlockSpec(shape, lambda i: (0,) * len(shape), pipeline_mode=pl.Buffered(1))
    return pl.pallas_call(
        _ffn1_proj_kernel,
        grid=(n // tm,),
        in_specs=[
            pl.BlockSpec((tm, d), lambda i: (i, 0)),
            const((1, d)), const((d, d_ff)), const((d, d_ff)), const((d_ff, d)), const((1, d)),
            const((1, d)), const((d, d_in)),
        ],
        out_specs=[
            pl.BlockSpec((tm, d), lambda i: (i, 0)),
            pl.BlockSpec((1, blocks_per_tile, D_ATTN, MOBA_BLOCK),
                         lambda i: (i // tiles_per_seq, i % tiles_per_seq, 0, 0)),
            pl.BlockSpec((tm, D_ATTN), lambda i: (i, 0)),
            pl.BlockSpec((1, blocks_per_tile, D_ATTN, MOBA_BLOCK),
                         lambda i: (i // tiles_per_seq, i % tiles_per_seq, 0, 0)),
            pl.BlockSpec((d_ssm // LANES, tm, LANES), lambda i: (0, i, 0)),
            pl.BlockSpec((1, blocks_per_tile, D_ATTN), lambda i: (i, 0, 0)),
        ],
        out_shape=[
            jax.ShapeDtypeStruct((n, d), F32),
            jax.ShapeDtypeStruct((bsz, nb, D_ATTN, MOBA_BLOCK), BF16),
            jax.ShapeDtypeStruct((n, D_ATTN), BF16),
            jax.ShapeDtypeStruct((bsz, nb, D_ATTN, MOBA_BLOCK), BF16),
            jax.ShapeDtypeStruct((d_ssm // LANES, n, LANES), F32),
            jax.ShapeDtypeStruct((n // tm, blocks_per_tile, D_ATTN), F32),
        ],
        scratch_shapes=[pltpu.VMEM((tm, d_ff), BF16)],
        compiler_params=pltpu.CompilerParams(
            dimension_semantics=("arbitrary",), vmem_limit_bytes=VMEM_LIMIT_BYTES),
        name="ffn1_proj",
    )(x2, pre_g, wg, wu, wd, post_g, mix_g, win)


def _tree_max_rows(t):
    while t.shape[0] > SUBLANES:
        half = t.shape[0] // 2
        t = jnp.maximum(t[:half], t[half:])
    return jnp.max(t, axis=0, keepdims=True)


def _moba_kernel(slopes_ref, qt_ref, k_ref, vt_ref, kmean_ref, *rest):
    n_cast = (len(rest) - 8) // 2
    cast_in, o_ref, cast_out = rest[:n_cast], rest[n_cast], rest[n_cast + 1:2 * n_cast + 1]
    qm_ref, diag_ref, feat_ref, s_ref, m_ref, l_ref, acc_ref = rest[2 * n_cast + 1:]
    for w_in_ref, w_out_ref in zip(cast_in, cast_out):
        w_out_ref[...] = w_in_ref[...].astype(w_out_ref.dtype)
    j = pl.program_id(1)
    blk = MOBA_BLOCK
    nb = k_ref.shape[1]
    kpos = lax.broadcasted_iota(jnp.int32, (blk, blk), 0)
    qpos = lax.broadcasted_iota(jnp.int32, (blk, blk), 1)
    nidx = lax.broadcasted_iota(jnp.int32, (nb, blk), 0)
    qcol = lax.broadcasted_iota(jnp.int32, (nb, blk), 1)
    chan = lax.broadcasted_iota(jnp.int32, (LANES, blk), 0)
    past = nidx < j
    ones = jnp.ones((BF16_SUBLANES, blk), BF16)

    def split3(x):
        hi = x.astype(BF16).astype(F32)
        mid = (x - hi).astype(BF16).astype(F32)
        return hi, mid, x - hi - mid

    @pl.when((pl.program_id(0) == 0) & (j == 0))
    def _():
        for h in range(N_HEADS):
            diag_ref[h] = jnp.where(qpos >= kpos, slopes_ref[h] * (kpos - qpos).astype(F32), NEG_INF)
        key_off = lax.broadcasted_iota(jnp.int32, (blk, LANES), 0).astype(F32)
        lane = lax.broadcasted_iota(jnp.int32, (blk, LANES), 1)
        for n in range(nb):
            onehot = (lane == n) | (lane == nb + n) | (lane == 2 * nb + n)
            is_off = (lane >= 3 * nb) & (lane < 3 * nb + 3)
            feat_ref[n] = jnp.where(onehot, 1.0, jnp.where(is_off, key_off, 0.0)).astype(BF16)

    def lane_tile(h):
        t = h // HEADS_PER_LANE_TILE
        return slice(t * LANES, (t + 1) * LANES)

    sub = lax.broadcasted_iota(jnp.int32, (SUBLANES, blk), 0)
    for h in range(N_HEADS):
        slope = slopes_ref[h]
        lanes = lane_tile(h)
        lo = (h % HEADS_PER_LANE_TILE) * HEAD_DIM
        qt2 = qt_ref[0, 0, lanes, :]
        qm = jnp.where((chan >= lo) & (chan < lo + HEAD_DIM), qt2, jnp.zeros_like(qt2))
        gate3 = jnp.dot(jnp.concatenate(split3(kmean_ref[0, :, lanes]), axis=0).astype(BF16), qm,
                        preferred_element_type=F32)
        gate = gate3[0:nb] + gate3[nb:2 * nb] + gate3[2 * nb:3 * nb]
        gate = jnp.where(past, gate, NEG_INF)
        rank = jnp.zeros((nb, blk), jnp.int32)
        for m in range(nb):
            gm = gate[m:m + 1, :]
            beats = (gm > gate) | ((gm == gate) & (m < nidx))
            rank = rank + beats.astype(jnp.int32)
        selected = past & (rank < MOBA_TOPK)
        rowbias = jnp.where(selected, 0.0, NEG_INF) - slope * ((j - nidx) * blk + qcol).astype(F32)
        s_hi, s_mid, s_lo = split3(jnp.full((SUBLANES, blk), slope, F32))
        slope_rows = jnp.where(sub == 0, s_hi, jnp.where(sub == 1, s_mid, jnp.where(sub == 2, s_lo, 0.0)))
        aug = jnp.concatenate(list(split3(rowbias)) + [slope_rows,
                              jnp.zeros((LANES - 3 * nb - SUBLANES, blk), F32)], axis=0)
        qm_ref[h] = jnp.concatenate([qm, aug.astype(BF16)], axis=0)
        m_ref[h] = jnp.full((1, blk), NEG_INF, F32)
        l_ref[h] = jnp.zeros((1, blk), F32)
        acc_ref[h] = jnp.zeros((HEAD_DIM, blk), F32)

    def scores_own(h):
        s_ref[h] = jnp.dot(k_ref[0, j, :, lane_tile(h)], qm_ref[h, 0:LANES, :], preferred_element_type=F32)

    def scores_past(n, h):
        lhs = jnp.concatenate([k_ref[0, n, :, lane_tile(h)], feat_ref[n]], axis=1)
        s_ref[h] = jnp.dot(lhs, qm_ref[h], preferred_element_type=F32)

    def accumulate(n, h, own):
        t = s_ref[h] + diag_ref[h] if own else s_ref[h]
        m = m_ref[h]
        m_new = jnp.maximum(m, _tree_max_rows(t))
        p = jnp.exp2(t - m_new)
        alpha = jnp.exp2(m - m_new)
        m_ref[h] = m_new
        lhs = jnp.concatenate([vt_ref[0, n, h * HEAD_DIM:(h + 1) * HEAD_DIM, :], ones], axis=0)
        pv = jnp.dot(lhs, p.astype(BF16), preferred_element_type=F32)
        l_ref[h] = alpha * l_ref[h] + pv[HEAD_DIM:HEAD_DIM + 1, :]
        acc_ref[h] = alpha * acc_ref[h] + pv[0:HEAD_DIM, :]

    def sweep(scores_this, scores_next, softmax_pv):
        for h in range(N_HEADS):
            ahead = h + SCORE_LOOKAHEAD
            if ahead < N_HEADS:
                scores_this(ahead)
            else:
                scores_next(ahead - N_HEADS)
            softmax_pv(h)

    for h in range(SCORE_LOOKAHEAD):
        scores_own(h)
    n_first = jnp.maximum(j - 1, 0)
    sweep(scores_own, functools.partial(scores_past, n_first),
          lambda h: accumulate(j, h, own=True))

    def body(i, carry):
        n = j - 1 - i
        n_next = jnp.maximum(n - 1, 0)
        sweep(functools.partial(scores_past, n), functools.partial(scores_past, n_next),
              lambda h: accumulate(n, h, own=False))
        return carry

    lax.fori_loop(0, j, body, 0)
    for t in range(N_HEADS // HEADS_PER_LANE_TILE):
        o_t = jnp.concatenate([acc_ref[h] / l_ref[h] for h in range(t * HEADS_PER_LANE_TILE,
                                                                     (t + 1) * HEADS_PER_LANE_TILE)], axis=0)
        o_ref[0, :, t * LANES:(t + 1) * LANES] = o_t.T.astype(o_ref.dtype)


def _cast_spec(rows, cols, steps, nb):
    n_blocks = next(n for n in (steps >> s for s in range(steps.bit_length()))
                    if steps % n == 0 and rows % n == 0 and (rows // n) % BF16_SUBLANES == 0)
    steps_per_block = steps // n_blocks
    return pl.BlockSpec((rows // n_blocks, cols), lambda b, j: ((b * nb + j) // steps_per_block, 0))


def _moba_attention(slopes, qt, k4, vt, kmean, cast_weights):
    bsz, nb, _, blk = qt.shape
    cast_specs = [_cast_spec(w.shape[0], w.shape[1], bsz * nb, nb) for w in cast_weights]
    return pl.pallas_call(
        _moba_kernel,
        grid=(bsz, nb),
        in_specs=[
            pl.BlockSpec(memory_space=pltpu.SMEM),
            pl.BlockSpec((1, 1, D_ATTN, blk), lambda b, j: (b, j, 0, 0)),
            pl.BlockSpec((1, nb, blk, D_ATTN), lambda b, j: (b, 0, 0, 0)),
            pl.BlockSpec((1, nb, D_ATTN, blk), lambda b, j: (b, 0, 0, 0)),
            pl.BlockSpec((1, nb, D_ATTN), lambda b, j: (b, 0, 0)),
        ] + cast_specs,
        out_specs=[pl.BlockSpec((1, blk, D_ATTN), lambda b, j: (b, j, 0))] + cast_specs,
        out_shape=[jax.ShapeDtypeStruct((bsz, nb * blk, D_ATTN), BF16)]
        + [jax.ShapeDtypeStruct(w.shape, BF16) for w in cast_weights],
        scratch_shapes=[
            pltpu.VMEM((N_HEADS, 2 * LANES, blk), BF16),
            pltpu.VMEM((N_HEADS, blk, blk), F32),
            pltpu.VMEM((nb, blk, LANES), BF16),
            pltpu.VMEM((N_HEADS, blk, blk), F32),
            pltpu.VMEM((N_HEADS, 1, blk), F32),
            pltpu.VMEM((N_HEADS, 1, blk), F32),
            pltpu.VMEM((N_HEADS, HEAD_DIM, blk), F32),
        ],
        compiler_params=pltpu.CompilerParams(
            dimension_semantics=("arbitrary", "arbitrary"), vmem_limit_bytes=ATTN_VMEM_LIMIT_BYTES),
        name="moba_attn",
    )(slopes, qt, k4, vt, kmean, *cast_weights)


def _s5_kernel(u_ref, wb_ref, are_ref, aim_ref, wc_ref, d_ref, y_ref, ubuf, ybuf, xbuf, hstate):
    n_tiles, bsz, steps, _ = u_ref.shape
    rows = bsz * steps
    u2 = u_ref.reshape(n_tiles, rows, LANES)
    y2 = y_ref.reshape(n_tiles, rows, LANES)

    @pl.when(pl.program_id(0) == 0)
    def _():
        hstate[...] = jnp.zeros_like(hstate)

    def gather_rows(lt):
        for t in range(steps):
            ubuf[t * bsz:(t + 1) * bsz, lt * LANES:(lt + 1) * LANES] = u2[lt, pl.ds(t, bsz, stride=steps), :]

    def scatter_rows(lt):
        for t in range(steps):
            y2[lt, pl.ds(t, bsz, stride=steps), :] = ybuf[t * bsz:(t + 1) * bsz, lt * LANES:(lt + 1) * LANES]

    def state_cols(lt):
        return slice(lt * STATE_COLS_PER_LANE_TILE, (lt + 1) * STATE_COLS_PER_LANE_TILE)

    def project_in(lt):
        u_t = ubuf[:, lt * LANES:(lt + 1) * LANES].astype(BF16)
        xbuf[:, state_cols(lt)] = jnp.dot(u_t, wb_ref[lt], preferred_element_type=F32)

    def scan(lt):
        pairs = range(lt * PAIRS_PER_LANE_TILE, (lt + 1) * PAIRS_PER_LANE_TILE)
        a_re = [jnp.broadcast_to(are_ref[pi:pi + 1, :], (SUBLANES, LANES)) for pi in pairs]
        a_im = [jnp.broadcast_to(aim_ref[pi:pi + 1, :], (SUBLANES, LANES)) for pi in pairs]
        cols = [pi * 2 * LANES for pi in pairs]
        hs = [(hstate[:, c:c + LANES], hstate[:, c + LANES:c + 2 * LANES]) for c in cols]
        for t in range(steps):
            r = slice(t * SUBLANES, (t + 1) * SUBLANES)
            for i, (c, ar, ai) in enumerate(zip(cols, a_re, a_im)):
                hr, hi = hs[i]
                nr = ar * hr - ai * hi + xbuf[r, c:c + LANES]
                ni = ar * hi + ai * hr + xbuf[r, c + LANES:c + 2 * LANES]
                xbuf[r, c:c + LANES] = nr
                xbuf[r, c + LANES:c + 2 * LANES] = ni
                hs[i] = (nr, ni)
        for c, (hr, hi) in zip(cols, hs):
            hstate[:, c:c + LANES] = hr
            hstate[:, c + LANES:c + 2 * LANES] = hi

    def project_out(lt):
        h_t = xbuf[:, state_cols(lt)].astype(BF16)
        lanes = slice(lt * LANES, (lt + 1) * LANES)
        ybuf[:, lanes] = (jnp.dot(h_t, wc_ref[lt], preferred_element_type=F32)
                          + d_ref[:, lanes] * ubuf[:, lanes])

    gather_rows(0)
    project_in(0)
    for lt in range(n_tiles):
        if lt + 1 < n_tiles:
            gather_rows(lt + 1)
            project_in(lt + 1)
        scan(lt)
        project_out(lt)
        if lt > 0:
            scatter_rows(lt - 1)
    scatter_rows(n_tiles - 1)


def _s5_weights(lam_re, lam_im, log_dt, b_re, b_im, c_re, c_im):
    g, p = lam_re.shape
    n_tiles = g // GROUPS_PER_LANE_TILE
    dt = jnp.exp(log_dt)[:, None]
    mag = jnp.exp(lam_re * dt)
    ang = lam_im * dt
    a_re = mag * jnp.cos(ang)
    a_im = mag * jnp.sin(ang)
    den = lam_re * lam_re + lam_im * lam_im
    f_re = ((a_re - 1.0) * lam_re + a_im * lam_im) / den
    f_im = (a_im * lam_re - (a_re - 1.0) * lam_im) / den
    bb_re = f_re[..., None] * b_re - f_im[..., None] * b_im
    bb_im = f_re[..., None] * b_im + f_im[..., None] * b_re
    gl = np.arange(GROUPS_PER_LANE_TILE)[:, None, None]
    delta = jnp.asarray(gl == 2 * np.arange(PAIRS_PER_LANE_TILE)[None, :, None] + np.arange(2)[None, None, :], F32)
    bbs = jnp.stack([bb_re, bb_im]).reshape(2, n_tiles, GROUPS_PER_LANE_TILE, p, SSM_GROUP)
    wb = jnp.einsum('rlgpi,gkj->lgikrjp', bbs, delta).reshape(n_tiles, LANES, STATE_COLS_PER_LANE_TILE)
    cs = jnp.stack([c_re, -c_im]).reshape(2, n_tiles, GROUPS_PER_LANE_TILE, SSM_GROUP, p)
    wc = jnp.einsum('rlgop,gkj->lkrjpgo', cs, delta).reshape(n_tiles, STATE_COLS_PER_LANE_TILE, LANES)
    n_pairs = g // 2
    return (wb.astype(BF16), a_re.reshape(n_pairs, 2 * p), a_im.reshape(n_pairs, 2 * p), wc.astype(BF16))


def _s5_scan(us, wb, a_re, a_im, wc, d_skip):
    n_tiles, bsz, seq, _ = us.shape
    d_ssm = n_tiles * LANES
    steps = SCAN_STEPS
    rows = steps * bsz
    state_cols = n_tiles * STATE_COLS_PER_LANE_TILE
    const = lambda shape: pl.BlockSpec(shape, lambda c: (0,) * len(shape))
    return pl.pallas_call(
        _s5_kernel,
        grid=(seq // steps,),
        in_specs=[
            pl.BlockSpec((n_tiles, bsz, steps, LANES), lambda c: (0, 0, c, 0)),
            const(wb.shape), const(a_re.shape), const(a_im.shape), const(wc.shape), const((1, d_ssm)),
        ],
        out_specs=pl.BlockSpec((n_tiles, bsz, steps, LANES), lambda c: (0, 0, c, 0)),
        out_shape=jax.ShapeDtypeStruct((n_tiles, bsz, seq, LANES), F32),
        scratch_shapes=[pltpu.VMEM((rows, d_ssm), F32), pltpu.VMEM((rows, d_ssm), F32),
                        pltpu.VMEM((rows, state_cols), F32), pltpu.VMEM((bsz, state_cols), F32)],
        compiler_params=pltpu.CompilerParams(
            dimension_semantics=("arbitrary",), vmem_limit_bytes=VMEM_LIMIT_BYTES),
        name="s5_scan",
    )(us, wb, a_re, a_im, wc, d_skip)


def _mix_ffn2_kernel(h_ref, attn_ref, y_ref, wglu, bglu, attn_g, ssm_g, wout, mixpost_g,
                     pre_g, wg, wu, wd, post_g, o_ref, act_ref):
    y = jnp.concatenate([y_ref[lt] for lt in range(y_ref.shape[0])], axis=-1)
    y = jax.nn.gelu(y, approximate=True)
    z = y * jax.nn.sigmoid(jnp.dot(y.astype(BF16), wglu[...], preferred_element_type=F32) + bglu[...])
    zs = _rms(z, ssm_g[...]).astype(BF16)
    at = _rms(attn_ref[...].astype(F32), attn_g[...]).astype(BF16)
    mixed = (jnp.dot(at, wout[0:D_ATTN, :], preferred_element_type=F32)
             + jnp.dot(zs, wout[D_ATTN:, :], preferred_element_type=F32))
    h = h_ref[...] + _rms(mixed, mixpost_g[...])
    f = _swiglu(_rms(h, pre_g[...]).astype(BF16), wg, wu, wd, act_ref)
    o_ref[...] = h + 0.5 * _rms(f, post_g[...])


def _mix_ffn2(h2, attn2, y2, wglu, bglu, attn_g, ssm_g, wout, mixpost_g, pre_g, wg, wu, wd, post_g, seq):
    n, d = h2.shape
    d_ff = wg.shape[1]
    d_ssm = wglu.shape[0]
    tm = ROW_TILE
    tiles_per_seq = seq // tm
    const = lambda shape: pl.BlockSpec(shape, lambda i: (0,) * len(shape), pipeline_mode=pl.Buffered(1))
    return pl.pallas_call(
        _mix_ffn2_kernel,
        grid=(n // tm,),
        in_specs=[
            pl.BlockSpec((tm, d), lambda i: (i, 0)),
            pl.BlockSpec((tm, D_ATTN), lambda i: (i, 0)),
            pl.BlockSpec((d_ssm // LANES, tm, LANES), lambda i: (0, i, 0)),
            const((d_ssm, d_ssm)), const((1, d_ssm)), const((1, D_ATTN)), const((1, d_ssm)),
            const((D_ATTN + d_ssm, d)), const((1, d)),
            const((1, d)), const((d, d_ff)), const((d, d_ff)), const((d_ff, d)), const((1, d)),
        ],
        out_specs=pl.BlockSpec((tm, d), lambda i: (i, 0)),
        out_shape=jax.ShapeDtypeStruct((n, d), F32),
        scratch_shapes=[pltpu.VMEM((tm, d_ff), BF16)],
        compiler_params=pltpu.CompilerParams(
            dimension_semantics=("arbitrary",), vmem_limit_bytes=VMEM_LIMIT_BYTES),
        name="mix_ffn2",
    )(h2, attn2, y2, wglu, bglu, attn_g, ssm_g, wout, mixpost_g, pre_g, wg, wu, wd, post_g)


def kernel(x, ffn1_pre_g, ffn1_w_gate, ffn1_w_up, ffn1_w_down, ffn1_post_g, mix_pre_g, w_in, lam_re, lam_im, log_dt, b_re, b_im, c_re, c_im, d_skip, w_glu, b_glu, attn_out_g, ssm_out_g, w_out, mix_post_g, ffn2_pre_g, ffn2_w_gate, ffn2_w_up, ffn2_w_down, ffn2_post_g):
    bsz, seq, d = x.shape
    depth = w_in.shape[0]
    d_ssm = w_glu.shape[1]
    assert w_in.shape[2] == 3 * D_ATTN + d_ssm and d_ssm == lam_re.shape[1] * SSM_GROUP
    assert seq % ROW_TILE == 0 and ROW_TILE % MOBA_BLOCK == 0 and seq % SCAN_STEPS == 0
    assert bsz == SUBLANES and lam_re.shape[2] == SSM_STATE
    nb = seq // MOBA_BLOCK
    slopes = jnp.asarray(LOG2E * 2.0 ** (-8.0 * np.arange(1, N_HEADS + 1) / N_HEADS), dtype=F32)
    row = lambda g: g.reshape(1, -1).astype(F32)
    bf = lambda w: w.astype(BF16)

    h2 = x.reshape(bsz * seq, d)
    for l in range(depth):
        h2, qt, k2, vt, us, kmean = _ffn1_proj(
            h2, row(ffn1_pre_g[l]), bf(ffn1_w_gate[l]), bf(ffn1_w_up[l]), bf(ffn1_w_down[l]),
            row(ffn1_post_g[l]), row(mix_pre_g[l]), bf(w_in[l]), bsz, seq)
        attn, wglu_bf, wout_bf, wg2_bf, wu2_bf, wd2_bf = _moba_attention(
            slopes, qt, k2.reshape(bsz, nb, MOBA_BLOCK, D_ATTN), vt, kmean.reshape(bsz, nb, D_ATTN),
            [w_glu[l], w_out[l], ffn2_w_gate[l], ffn2_w_up[l], ffn2_w_down[l]])
        wb, a_re, a_im, wc = _s5_weights(lam_re[l], lam_im[l], log_dt[l], b_re[l], b_im[l], c_re[l], c_im[l])
        y = _s5_scan(us.reshape(-1, bsz, seq, LANES), wb, a_re, a_im, wc, row(d_skip[l]))
        h2 = _mix_ffn2(
            h2, attn.reshape(bsz * seq, D_ATTN), y.reshape(-1, bsz * seq, LANES),
            wglu_bf, row(b_glu[l]), row(attn_out_g[l]), row(ssm_out_g[l]), wout_bf,
            row(mix_post_g[l]), row(ffn2_pre_g[l]), wg2_bf, wu2_bf, wd2_bf, row(ffn2_post_g[l]), seq)
    return h2.reshape(bsz, seq, d)
```
